```python
import jax
import jax.numpy as jnp
from jax import lax
import numpy as np

D_MODEL = 1024
BATCH = 16
SEQ = 256
DEPTH = 4
DEC_BATCH = 4
DEC_SEQ = 1024
PAST_LEN = 256

GRID_W = 64
A_HEADS = 4
A_NOPE = 64
A_ROPE = 32
A_V = 64
A_QK = A_NOPE + A_ROPE
A_Q_RANK = 256
A_KV_RANK = 128
B_HEADS = 4
B_KV_HEADS = 2
B_HEAD_DIM = 64
B_WINDOW = 128
B_BLOCK = 128
C_GROUPS = 4
C_GROUP_DIM = 64
C_WIDTH = C_GROUPS * C_GROUP_DIM
D_HEADS = 4
D_HEAD_DIM = 64
NA_ROWS = 8
NA_COLS = 16

MIX_WIDTH = A_HEADS * A_V + B_HEADS * B_HEAD_DIM + C_WIDTH + D_HEADS * D_HEAD_DIM
IN_SIZES = (A_Q_RANK, A_KV_RANK, A_ROPE,
            B_HEADS * B_HEAD_DIM, B_KV_HEADS * B_HEAD_DIM, B_KV_HEADS * B_HEAD_DIM,
            C_WIDTH,
            D_HEADS * D_HEAD_DIM, D_HEADS * D_HEAD_DIM, D_HEADS * D_HEAD_DIM)
IN_WIDTH = sum(IN_SIZES)
D_FF = 2816
N_MOD = 9
ROPE_BASE = 10000.0
EPS = 1e-6
DENSE_QBLOCK = 128
DENSE_SWEEP_KEYS = 2048
F32 = jnp.float32

kernel_name = 'hybrid_prefix_diffusion_step'


def rms_norm(x, g):
    xf = x.astype(F32)
    y = xf * lax.rsqrt(jnp.mean(xf * xf, axis=-1, keepdims=True) + EPS)
    return (y * g.astype(F32)).astype(x.dtype)


def swiglu(x, w_gate, w_up, w_down):
    return (jax.nn.silu(x @ w_gate) * (x @ w_up)) @ w_down


def split_cols(x, sizes):
    out, start = [], 0
    for size in sizes:
        out.append(x[..., start:start + size])
        start += size
    return out


def axial_rope(x):
    s, r = x.shape[1], x.shape[-1]
    half = r // 2
    t = jnp.arange(s)
    inv = ROPE_BASE ** (-jnp.arange(0, half, 2, dtype=F32) / half)

    def rot(xa, pos):
        ang = pos.astype(F32)[:, None] * inv[None, :]
        cos = jnp.cos(ang)[None, :, None, :]
        sin = jnp.sin(ang)[None, :, None, :]
        x1, x2 = jnp.split(xa.astype(F32), 2, axis=-1)
        return jnp.concatenate([x1 * cos - x2 * sin, x2 * cos + x1 * sin], axis=-1)

    out = jnp.concatenate([rot(x[..., :half], t // GRID_W), rot(x[..., half:], t % GRID_W)], axis=-1)
    return out.astype(x.dtype)


def dense_attend(q, k, v, sink=None):
    b, lq, h, d = q.shape
    hk, dv = k.shape[2], v.shape[-1]
    g = h // hk
    scale = d ** -0.5
    kf, vf = k.astype(F32), v.astype(F32)

    def attend(qb):
        s = jnp.einsum('bqkgd,bskd->bkgqs', qb.astype(F32), kf) * scale
        if sink is not None:
            sk = jnp.broadcast_to(sink.astype(F32).reshape(1, hk, g, 1, 1), s.shape[:-1] + (1,))
            p = jax.nn.softmax(jnp.concatenate([s, sk], axis=-1), axis=-1)[..., :-1]
        else:
            p = jax.nn.softmax(s, axis=-1)
        return jnp.einsum('bkgqs,bskd->bqkgd', p, vf)

    qg = q.reshape(b, lq, hk, g, d)
    if k.shape[1] >= DENSE_SWEEP_KEYS and lq % DENSE_QBLOCK == 0:
        qs = jnp.moveaxis(qg.reshape(b, lq // DENSE_QBLOCK, DENSE_QBLOCK, hk, g, d), 1, 0)
        o = jnp.moveaxis(lax.map(attend, qs), 0, 1)
    else:
        o = attend(qg)
    return o.reshape(b, lq, h, dv).astype(q.dtype)


def window_attend(q, k, v, kc, vc, sink):
    b, s, h, d = q.shape
    hk = k.shape[2]
    g = h // hk
    nb = s // B_BLOCK
    scale = d ** -0.5
    qb = q.reshape(b, nb, B_BLOCK, hk, g, d).astype(F32)
    pad = ((0, 0), (B_BLOCK, B_BLOCK), (0, 0), (0, 0))
    kp = jnp.pad(k.astype(F32), pad).reshape(b, nb + 2, B_BLOCK, hk, d)
    vp = jnp.pad(v.astype(F32), pad).reshape(b, nb + 2, B_BLOCK, hk, d)
    kw = jnp.concatenate([kp[:, :nb], kp[:, 1:nb + 1], kp[:, 2:]], axis=2)
    vw = jnp.concatenate([vp[:, :nb], vp[:, 1:nb + 1], vp[:, 2:]], axis=2)
    qpos = jnp.arange(s).reshape(nb, B_BLOCK)
    kpos = (jnp.arange(nb)[:, None] - 1) * B_BLOCK + jnp.arange(3 * B_BLOCK)[None, :]
    valid = ((kpos[:, None, :] >= 0) & (kpos[:, None, :] < s)
             & (jnp.abs(qpos[:, :, None] - kpos[:, None, :]) <= B_WINDOW))
    s_loc = jnp.einsum('bnqkgd,bnskd->bnkgqs', qb, kw) * scale
    s_loc = jnp.where(valid[None, :, None, None], s_loc, -jnp.inf)
    s_ctx = jnp.einsum('bnqkgd,bckd->bnkgqc', qb, kc.astype(F32)) * scale
    sk = jnp.broadcast_to(sink.astype(F32).reshape(1, 1, hk, g, 1, 1), s_loc.shape[:-1] + (1,))
    p = jax.nn.softmax(jnp.concatenate([s_loc, s_ctx, sk], axis=-1), axis=-1)
    n_loc, n_ctx = 3 * B_BLOCK, kc.shape[1]
    o = (jnp.einsum('bnkgqs,bnskd->bnqkgd', p[..., :n_loc], vw)
         + jnp.einsum('bnkgqc,bckd->bnqkgd', p[..., n_loc:n_loc + n_ctx], vc.astype(F32)))
    return o.reshape(b, s, h, d).astype(q.dtype)


def neighbourhood_attend(q, k, v, kc, vc, rpb):
    b, s, h, d = q.shape
    rows = s // GRID_W
    kr = min(NA_ROWS, rows)
    scale = d ** -0.5
    qg = q.reshape(b, rows, GRID_W, h, d).astype(F32)
    kg = k.reshape(b, rows, GRID_W, h, d).astype(F32)
    vg = v.reshape(b, rows, GRID_W, h, d).astype(F32)
    r = jnp.arange(rows)
    row_idx = jnp.clip(r - kr // 2, 0, rows - kr)[:, None] + jnp.arange(kr)[None, :]
    k_blk = kg[:, row_idx]
    v_blk = vg[:, row_idx]
    col = jnp.arange(GRID_W)
    col_start = jnp.clip(col - NA_COLS // 2, 0, GRID_W - NA_COLS)
    col_ok = (col[None, :] >= col_start[:, None]) & (col[None, :] < col_start[:, None] + NA_COLS)
    dr_idx = row_idx - r[:, None] + NA_ROWS - 1
    dc_idx = jnp.clip(col[None, :] - col[:, None] + NA_COLS - 1, 0, 2 * NA_COLS - 2)
    bias = rpb.astype(F32)[:, dr_idx[:, None, :, None], dc_idx[None, :, None, :]]
    s_loc = jnp.einsum('brqhd,brkwhd->bhrqkw', qg, k_blk) * scale + bias[None]
    s_loc = jnp.where(col_ok[:, None, :], s_loc, -jnp.inf)
    n_loc = kr * GRID_W
    s_loc = s_loc.reshape(b, h, rows, GRID_W, n_loc)
    s_ctx = jnp.einsum('brqhd,bchd->bhrqc', qg, kc.astype(F32)) * scale
    p = jax.nn.softmax(jnp.concatenate([s_loc, s_ctx], axis=-1), axis=-1)
    p_loc = p[..., :n_loc].reshape(b, h, rows, GRID_W, kr, GRID_W)
    o = (jnp.einsum('bhrqkw,brkwhd->brqhd', p_loc, v_blk)
         + jnp.einsum('bhrqc,bchd->brqhd', p[..., n_loc:], vc.astype(F32)))
    return o.reshape(b, s, h, d).astype(q.dtype)


def fourier_mix(x):
    b, l, _ = x.shape
    xg = x.reshape(b, l, C_GROUPS, C_GROUP_DIM).astype(F32)
    y = jnp.fft.fft2(xg, axes=(1, 3), norm='ortho').real
    return y.reshape(b, l, C_WIDTH).astype(x.dtype)


def mla_queries(cq, lp, rope):
    b, l, _ = cq.shape
    q = (rms_norm(cq, lp['g_qa']) @ lp['w_uq']).reshape(b, l, A_HEADS, A_QK)
    q = rms_norm(q, lp['qn_a'])
    if rope:
        q = jnp.concatenate([q[..., :A_NOPE], axial_rope(q[..., A_NOPE:])], axis=-1)
    return q


def mla_keys_values(ckv_n, krope, lp, rope):
    b, l, _ = ckv_n.shape
    kv = (ckv_n @ lp['w_ukv']).reshape(b, l, A_HEADS, A_NOPE + A_V)
    k = jnp.concatenate([kv[..., :A_NOPE],
                         jnp.broadcast_to(krope[:, :, None, :], (b, l, A_HEADS, A_ROPE))], axis=-1)
    k = rms_norm(k, lp['kn_a'])
    if rope:
        k = jnp.concatenate([k[..., :A_NOPE], axial_rope(k[..., A_NOPE:])], axis=-1)
    return k, kv[..., A_NOPE:]


def merge_heads(oa, ob, oc, od):
    b, l = oc.shape[:2]
    return jnp.concatenate([oa.reshape(b, l, -1), ob.reshape(b, l, -1), oc, od.reshape(b, l, -1)], axis=-1)


def mix_context(h, lp):
    b, l, _ = h.shape
    cq, ckv, kr, qb, kb, vb, xc, qd, kd, vd = split_cols(h @ lp['w_in'], IN_SIZES)
    ckv_n = rms_norm(ckv, lp['g_kva'])
    qa = mla_queries(cq, lp, False)
    ka, va = mla_keys_values(ckv_n, kr, lp, False)
    oa = dense_attend(qa, ka, va)
    qb = rms_norm(qb.reshape(b, l, B_HEADS, B_HEAD_DIM), lp['qn_b'])
    kb = rms_norm(kb.reshape(b, l, B_KV_HEADS, B_HEAD_DIM), lp['kn_b'])
    vb = vb.reshape(b, l, B_KV_HEADS, B_HEAD_DIM)
    ob = dense_attend(qb, kb, vb, sink=lp['sink_b'])
    oc = fourier_mix(xc)
    qd = rms_norm(qd.reshape(b, l, D_HEADS, D_HEAD_DIM), lp['qn_d'])
    kd = rms_norm(kd.reshape(b, l, D_HEADS, D_HEAD_DIM), lp['kn_d'])
    vd = vd.reshape(b, l, D_HEADS, D_HEAD_DIM)
    od = dense_attend(qd, kd, vd)
    o = merge_heads(oa, ob, oc, od) @ lp['w_o']
    return o, (ckv_n, kr, kb, vb, kd, vd)


def mix_latent(h, lp, ctx):
    ckv_c, kr_c, kb_c, vb_c, kd_c, vd_c = ctx
    b, s, _ = h.shape
    cq, ckv, kr, qb, kb, vb, xc, qd, kd, vd = split_cols(h @ lp['w_in'], IN_SIZES)
    qa = mla_queries(cq, lp, True)
    ka, va = mla_keys_values(rms_norm(ckv, lp['g_kva']), kr, lp, True)
    ka_c, va_c = mla_keys_values(ckv_c, kr_c, lp, False)
    oa = dense_attend(qa, jnp.concatenate([ka_c, ka], axis=1), jnp.concatenate([va_c, va], axis=1))
    qb = axial_rope(rms_norm(qb.reshape(b, s, B_HEADS, B_HEAD_DIM), lp['qn_b']))
    kb = axial_rope(rms_norm(kb.reshape(b, s, B_KV_HEADS, B_HEAD_DIM), lp['kn_b']))
    ob = window_attend(qb, kb, vb.reshape(b, s, B_KV_HEADS, B_HEAD_DIM), kb_c, vb_c, lp['sink_b'])
    oc = fourier_mix(xc)
    qd = rms_norm(qd.reshape(b, s, D_HEADS, D_HEAD_DIM), lp['qn_d'])
    kd = rms_norm(kd.reshape(b, s, D_HEADS, D_HEAD_DIM), lp['kn_d'])
    od = neighbourhood_attend(qd, kd, vd.reshape(b, s, D_HEADS, D_HEAD_DIM), kd_c, vd_c, lp['rpb_d'])
    o = merge_heads(oa, ob, oc, od) @ lp['w_o']
    return o, ()


def trunk_layer(x, mod, lp, mixer, *mixer_args):
    sh1, sc1, g1, sh2, sc2, g2, sh3, sc3, g3 = jnp.split(mod[:, None, :], N_MOD, axis=-1)
    h = rms_norm(x, lp['g_ffn1']) * (1 + sc1) + sh1
    x = x + 0.5 * g1 * swiglu(h, lp['w_gate1'], lp['w_up1'], lp['w_down1'])
    h = rms_norm(x, lp['g_mix']) * (1 + sc2) + sh2
    o, aux = mixer(h, lp, *mixer_args)
    x = x + g2 * o
    h = rms_norm(x, lp['g_ffn2']) * (1 + sc3) + sh3
    x = x + 0.5 * g3 * swiglu(h, lp['w_gate2'], lp['w_up2'], lp['w_down2'])
    return x, aux


def setup_inputs(seed: int = 0) -> dict:
    key = jax.random.key(seed)
    keys = iter(jax.random.split(key, 48))

    def nrm(shape, scale):
        return jax.random.normal(next(keys), shape, F32) * scale

    def gain(shape):
        return 1.0 + 0.02 * jax.random.normal(next(keys), shape, F32)

    return {
        'x_prompt': nrm((BATCH, SEQ, D_MODEL), 1.0),
        'x_sample': nrm((DEC_BATCH, DEC_SEQ, D_MODEL), 1.0),
        'cache_mla_ckv': nrm((DEC_BATCH, DEPTH, PAST_LEN, A_KV_RANK), 1.0),
        'cache_mla_krope': nrm((DEC_BATCH, DEPTH, PAST_LEN, A_ROPE), 1.0),
        'cache_win_k': nrm((DEC_BATCH, DEPTH, PAST_LEN, B_KV_HEADS, B_HEAD_DIM), 1.0),
        'cache_win_v': nrm((DEC_BATCH, DEPTH, PAST_LEN, B_KV_HEADS, B_HEAD_DIM), 1.0),
        'cache_na_k': nrm((DEC_BATCH, DEPTH, PAST_LEN, D_HEADS, D_HEAD_DIM), 1.0),
        'cache_na_v': nrm((DEC_BATCH, DEPTH, PAST_LEN, D_HEADS, D_HEAD_DIM), 1.0),
        'c': nrm((DEC_BATCH, D_MODEL), 1.0),
        'c_ctx': nrm((D_MODEL,), 1.0),
        'w_ada': nrm((DEPTH, D_MODEL, N_MOD * D_MODEL), 0.5 * D_MODEL ** -0.5),
        'b_ada': nrm((DEPTH, N_MOD * D_MODEL), 0.02),
        'g_ffn1': gain((DEPTH, D_MODEL)),
        'w_gate1': nrm((DEPTH, D_MODEL, D_FF), D_MODEL ** -0.5),
        'w_up1': nrm((DEPTH, D_MODEL, D_FF), D_MODEL ** -0.5),
        'w_down1': nrm((DEPTH, D_FF, D_MODEL), D_FF ** -0.5),
        'g_mix': gain((DEPTH, D_MODEL)),
        'w_in': nrm((DEPTH, D_MODEL, IN_WIDTH), D_MODEL ** -0.5),
        'g_qa': gain((DEPTH, A_Q_RANK)),
        'w_uq': nrm((DEPTH, A_Q_RANK, A_HEADS * A_QK), A_Q_RANK ** -0.5),
        'g_kva': gain((DEPTH, A_KV_RANK)),
        'w_ukv': nrm((DEPTH, A_KV_RANK, A_HEADS * (A_NOPE + A_V)), A_KV_RANK ** -0.5),
        'qn_a': gain((DEPTH, A_QK)),
        'kn_a': gain((DEPTH, A_QK)),
        'qn_b': gain((DEPTH, B_HEAD_DIM)),
        'kn_b': gain((DEPTH, B_HEAD_DIM)),
        'sink_b': nrm((DEPTH, B_HEADS), 0.5),
        'qn_d': gain((DEPTH, D_HEAD_DIM)),
        'kn_d': gain((DEPTH, D_HEAD_DIM)),
        'rpb_d': nrm((DEPTH, D_HEADS, 2 * NA_ROWS - 1, 2 * NA_COLS - 1), 0.2),
        'w_o': nrm((DEPTH, MIX_WIDTH, D_MODEL), MIX_WIDTH ** -0.5),
        'g_ffn2': gain((DEPTH, D_MODEL)),
        'w_gate2': nrm((DEPTH, D_MODEL, D_FF), D_MODEL ** -0.5),
        'w_up2': nrm((DEPTH, D_MODEL, D_FF), D_MODEL ** -0.5),
        'w_down2': nrm((DEPTH, D_FF, D_MODEL), D_FF ** -0.5),
    }


def reference(x_prompt, x_sample, cache_mla_ckv, cache_mla_krope, cache_win_k, cache_win_v,
              cache_na_k, cache_na_v, c, c_ctx, w_ada, b_ada, g_ffn1, w_gate1, w_up1, w_down1,
              g_mix, w_in, g_qa, w_uq, g_kva, w_ukv, qn_a, kn_a, qn_b, kn_b, sink_b, qn_d, kn_d,
              rpb_d, w_o, g_ffn2, w_gate2, w_up2, w_down2):
    y_prompt, y_sample = x_prompt, x_sample
    new_ckv, new_krope, new_wk, new_wv, new_nk, new_nv = [], [], [], [], [], []
    for i in range(DEPTH):
        lp = dict(g_ffn1=g_ffn1[i], w_gate1=w_gate1[i], w_up1=w_up1[i], w_down1=w_down1[i],
                  g_mix=g_mix[i], w_in=w_in[i], g_qa=g_qa[i], w_uq=w_uq[i], g_kva=g_kva[i],
                  w_ukv=w_ukv[i], qn_a=qn_a[i], kn_a=kn_a[i], qn_b=qn_b[i], kn_b=kn_b[i],
                  sink_b=sink_b[i], qn_d=qn_d[i], kn_d=kn_d[i], rpb_d=rpb_d[i], w_o=w_o[i],
                  g_ffn2=g_ffn2[i], w_gate2=w_gate2[i], w_up2=w_up2[i], w_down2=w_down2[i])
        mod_ctx = jax.nn.silu(c_ctx[None, :]) @ w_ada[i] + b_ada[i]
        mod_lat = jax.nn.silu(c) @ w_ada[i] + b_ada[i]
        y_prompt, (ckv_n, kr, kb, vb, kd, vd) = trunk_layer(y_prompt, mod_ctx, lp, mix_context)
        new_ckv.append(ckv_n)
        new_krope.append(kr)
        new_wk.append(kb)
        new_wv.append(vb)
        new_nk.append(kd)
        new_nv.append(vd)
        ctx = (cache_mla_ckv[:, i], cache_mla_krope[:, i], cache_win_k[:, i], cache_win_v[:, i],
               cache_na_k[:, i], cache_na_v[:, i])
        y_sample, _ = trunk_layer(y_sample, mod_lat, lp, mix_latent, ctx)
    return (y_prompt, y_sample, jnp.stack(new_ckv, axis=1), jnp.stack(new_krope, axis=1),
            jnp.stack(new_wk, axis=1), jnp.stack(new_wv, axis=1),
            jnp.stack(new_nk, axis=1), jnp.stack(new_nv, axis=1))
```

```python
import functools

import numpy as np
import jax
import jax.numpy as jnp
from jax import lax
from jax.experimental import pallas as pl
from jax.experimental.pallas import tpu as pltpu

F32 = jnp.float32
BF16 = jnp.bfloat16

D_MODEL = 1024
DEPTH = 4
SEQ = 256
DEC_SEQ = 1024
PAST_LEN = 256
GRID_W = 64
D_FF = 2816
N_MOD = 9
EPS = 1e-6
ROPE_BASE = 10000.0
A_QK = 96
A_ROPE = 32
B_WINDOW = 128
NA_ROWS = 8
NA_COLS = 16
HEAD_DIM = 64
N_HEADS = 4
NEG = -1e30

LANES = 128
MOD_ROWS = 8
VMEM_LIMIT = 56 * 1024 * 1024

QA, KA, VA = 0, 512, 1024
QB, KB, VB = 1280, 1536, 1664
XC = 1792
QD, KD, VD = 2048, 2304, 2560
FEAT = 2816
B_ORDER = (0, 2, 1, 3)

P_CQ, P_CKV, P_KR, P_QB, P_KB, P_VB, P_XC, P_QD, P_KD, P_VD = (
    0, 256, 384, 512, 768, 896, 1024, 1280, 1536, 1792)
P_WIDTH = 2048

Q_BLOCK = 256
B_KEYS = 512
D_KEY_ROWS = 12


def _cparams(sem):
    return pltpu.CompilerParams(dimension_semantics=sem, vmem_limit_bytes=VMEM_LIMIT)


def _dot(a, b):
    return jnp.dot(a, b, preferred_element_type=F32)


def _dot_t(a, b):
    return lax.dot_general(a, b, (((1,), (1,)), ((), ())), preferred_element_type=F32)


def _group_sums(sq, ones_bf16):
    hi = sq.astype(BF16)
    lo = (sq - hi.astype(F32)).astype(BF16)
    return _dot(hi, ones_bf16) + _dot(lo, ones_bf16)


def _rope_tables(width, slot, n_pos):
    half = width // 2
    quarter = half // 2
    inv = (np.float32(ROPE_BASE) ** (-np.arange(0, half, 2, dtype=np.float32) / np.float32(half))).astype(np.float32)
    t = np.arange(n_pos)
    pos = np.stack([t // GRID_W, t % GRID_W], axis=0).astype(np.float32)
    cos = np.ones((n_pos, LANES), np.float32)
    sin = np.zeros((n_pos, LANES), np.float32)
    for lane in range(LANES):
        i = lane % slot
        if i >= width:
            continue
        axis = i // half
        j = i % half
        ang = pos[axis] * inv[j % quarter]
        cos[:, lane] = np.cos(ang)
        sin[:, lane] = np.sin(ang) * (-1.0 if j < quarter else 1.0)
    return jnp.asarray(cos), jnp.asarray(sin)


def _dft_tables(n):
    k = np.arange(n)
    ang = 2.0 * np.pi * ((k[:, None] * k[None, :]) % n) / n
    return np.cos(ang), np.sin(ang)


def _fourier_constants(n_pos):
    cl, sl = _dft_tables(n_pos)
    cc, sc = _dft_tables(HEAD_DIM)
    scale = 1.0 / np.sqrt(float(HEAD_DIM * n_pos))
    eye = np.eye(N_HEADS)
    ccb = np.kron(eye, cc) * scale
    scb = np.kron(eye, sc) * scale
    as_bf = lambda a: jnp.asarray(a.astype(np.float32)).astype(BF16)
    return as_bf(cl), as_bf(sl), as_bf(ccb), as_bf(scb)


def _group_ones(width, group):
    g = np.arange(width) // group
    return jnp.asarray((g[:, None] == g[None, :]).astype(np.float32)).astype(BF16)


def _ada_kernel(cond_ref, w_ref, b_ref, o_ref):
    c = cond_ref[...]
    s = (c * jax.nn.sigmoid(c)).astype(BF16)
    o_ref[...] = _dot(s, w_ref[...].astype(BF16)) + b_ref[...]


def _ada_call(cond, w_ada, b_ada):
    tn = 1024
    n = N_MOD * D_MODEL
    return pl.pallas_call(
        _ada_kernel,
        grid=(DEPTH, n // tn),
        in_specs=[
            pl.BlockSpec((MOD_ROWS, D_MODEL), lambda l, j: (0, 0)),
            pl.BlockSpec((None, D_MODEL, tn), lambda l, j: (l, 0, j)),
            pl.BlockSpec((None, 1, tn), lambda l, j: (l, 0, j)),
        ],
        out_specs=pl.BlockSpec((None, MOD_ROWS, tn), lambda l, j: (l, 0, j)),
        out_shape=jax.ShapeDtypeStruct((DEPTH, MOD_ROWS, n), F32),
        compiler_params=_cparams(("parallel", "parallel")),
        name="ada_mod",
    )(cond, w_ada, b_ada.reshape(DEPTH, 1, n))


def _rpb_kernel(rpb_ref, o_ref):
    g = pl.program_id(0)
    n_dr = 2 * NA_ROWS - 1
    n_dc = 2 * NA_COLS - 1
    base = g * (n_dr * n_dc)
    shape = (GRID_W, LANES)
    q = lax.broadcasted_iota(jnp.int32, shape, 0)
    lane = lax.broadcasted_iota(jnp.int32, shape, 1)
    w = lane & (GRID_W - 1)
    dc = jnp.clip(w - q + (NA_COLS - 1), 0, n_dc - 1)
    cs = jnp.clip(q - NA_COLS // 2, 0, GRID_W - NA_COLS)
    col_ok = (w >= cs) & (w < cs + NA_COLS)
    neg = jnp.full(shape, NEG, F32)
    tiles = []
    for d in range(n_dr):
        acc = jnp.zeros(shape, F32)
        for j in range(n_dc):
            acc = jnp.where(dc == j, rpb_ref[base + d * n_dc + j], acc)
        tiles.append(jnp.where(col_ok, acc, neg))
    first = lane < GRID_W
    for e in range(n_dr + 1):
        left = tiles[e - 1] if e >= 1 else neg
        right = tiles[e] if e < n_dr else neg
        o_ref[e] = jnp.where(first, left, right)


def _rpb_call(rpb_d):
    n_e = 2 * NA_ROWS
    return pl.pallas_call(
        _rpb_kernel,
        grid=(DEPTH * N_HEADS,),
        in_specs=[pl.BlockSpec(memory_space=pltpu.SMEM)],
        out_specs=pl.BlockSpec((None, n_e, GRID_W, LANES), lambda g: (g, 0, 0, 0)),
        out_shape=jax.ShapeDtypeStruct((DEPTH * N_HEADS, n_e, GRID_W, LANES), F32),
        compiler_params=_cparams(("parallel",)),
        name="rpb_tiles",
    )(rpb_d.reshape(-1))


def _ffn_kernel(*refs, row0, row_stride, n_ff, with_mix):
    if with_mix:
        (x_ref, mix_ref, wo_ref, g2_ref, sh_ref, sc_ref, g_ref, gn_ref,
         wg_ref, wu_ref, wd_ref, o_ref, x_scr, h_scr, acc_scr) = refs
    else:
        (x_ref, sh_ref, sc_ref, g_ref, gn_ref,
         wg_ref, wu_ref, wd_ref, o_ref, x_scr, h_scr, acc_scr) = refs
    i = pl.program_id(0)
    j = pl.program_id(1)
    row = row0 + row_stride * i

    @pl.when(j == 0)
    def _():
        x = x_ref[...]
        if with_mix:
            x = x + g2_ref[pl.ds(row, 1), :] * _dot(mix_ref[...], wo_ref[...])
        x_scr[...] = x
        y = x * lax.rsqrt(jnp.mean(x * x, axis=-1, keepdims=True) + EPS) * gn_ref[...]
        h = y * (1.0 + sc_ref[pl.ds(row, 1), :]) + sh_ref[pl.ds(row, 1), :]
        h_scr[...] = h.astype(BF16)
        acc_scr[...] = jnp.zeros_like(acc_scr)

    h = h_scr[...]
    gate = _dot(h, wg_ref[...].astype(BF16))
    up = _dot(h, wu_ref[...].astype(BF16))
    a = (gate * jax.nn.sigmoid(gate)) * up
    acc_scr[...] += _dot(a.astype(BF16), wd_ref[...].astype(BF16))

    @pl.when(j == n_ff - 1)
    def _():
        o_ref[...] = x_scr[...] + (0.5 * g_ref[pl.ds(row, 1), :]) * acc_scr[...]


def _ffn_call(x, mod, layer, slot, g_norm, w_gate, w_up, w_down, *, latent, mix=None, w_o=None):
    t = x.shape[0]
    tm = 1024
    tf = 256
    n_ff = D_FF // tf
    row0, row_stride = (1, 1) if latent else (0, 0)
    with_mix = mix is not None
    mod_spec = lambda k: pl.BlockSpec((None, MOD_ROWS, D_MODEL), lambda i, j: (layer, 0, k))
    x_spec = pl.BlockSpec((tm, D_MODEL), lambda i, j: (i, 0))
    in_specs = [x_spec]
    args = [x]
    if with_mix:
        in_specs += [x_spec, pl.BlockSpec((None, D_MODEL, D_MODEL), lambda i, j: (layer, 0, 0)), mod_spec(5)]
        args += [mix, w_o, mod]
    c0 = 3 * slot
    in_specs += [mod_spec(c0), mod_spec(c0 + 1), mod_spec(c0 + 2),
                 pl.BlockSpec((None, 1, D_MODEL), lambda i, j: (layer, 0, 0)),
                 pl.BlockSpec((None, D_MODEL, tf), lambda i, j: (layer, 0, j)),
                 pl.BlockSpec((None, D_MODEL, tf), lambda i, j: (layer, 0, j)),
                 pl.BlockSpec((None, tf, D_MODEL), lambda i, j: (layer, j, 0))]
    args += [mod, mod, mod, g_norm.reshape(DEPTH, 1, D_MODEL), w_gate, w_up, w_down]
    return pl.pallas_call(
        functools.partial(_ffn_kernel, row0=row0, row_stride=row_stride, n_ff=n_ff, with_mix=with_mix),
        grid=(t // tm, n_ff),
        in_specs=in_specs,
        out_specs=x_spec,
        out_shape=jax.ShapeDtypeStruct((t, D_MODEL), F32),
        scratch_shapes=[pltpu.VMEM((tm, D_MODEL), F32), pltpu.VMEM((tm, D_MODEL), BF16),
                        pltpu.VMEM((tm, D_MODEL), F32)],
        compiler_params=_cparams(("parallel", "arbitrary")),
        name="ffn_mix" if with_mix else "ffn",
    )(*args)


def _swap(x, width):
    lane = lax.broadcasted_iota(jnp.int32, x.shape, 1)
    first = (lane & width) == 0
    return jnp.where(first, pltpu.roll(x, LANES - width, 1), pltpu.roll(x, width, 1))


def _rope(x, cos, sin, width):
    parts = []
    for c in range(x.shape[1] // LANES):
        xc = x[:, c * LANES:(c + 1) * LANES]
        parts.append(xc * cos + _swap(xc, width) * sin)
    return parts[0] if len(parts) == 1 else jnp.concatenate(parts, axis=1)


def _head_norm(x, ones, gain, inv_dim):
    ms = _group_sums(x * x, ones) * inv_dim
    return x * lax.rsqrt(ms + EPS) * gain


def _mla_keys(ckv_n_bf16, kr_pad, w_ukv, ones128, kn_a):
    kv = _dot(ckv_n_bf16, w_ukv)
    k = kv[:, :512] + jnp.concatenate([kr_pad] * N_HEADS, axis=1)
    return _head_norm(k, ones128, kn_a, 1.0 / A_QK), kv[:, 512:]


def _prep_kernel(*refs, row0, blocks_per_row, latent):
    (x_ref, sh_ref, sc_ref, gn_ref, win_ref, gqa_ref, wuq_ref, gkva_ref, wukv_ref,
     qna_ref, kna_ref, qnb_ref, knb_ref, qnd_ref, knd_ref,
     ones128_ref, ones64_ref) = refs[:17]
    if latent:
        cos_a_ref, sin_a_ref, cos_b_ref, sin_b_ref, feat_ref = refs[17:]
        row = row0 + pl.program_id(0) // blocks_per_row
    else:
        feat_ref, ckv_ref, kr_ref, wk_ref, wv_ref, nk_ref, nv_ref = refs[17:]
        row = row0

    x = x_ref[...]
    y = x * lax.rsqrt(jnp.mean(x * x, axis=-1, keepdims=True) + EPS) * gn_ref[...]
    h = (y * (1.0 + sc_ref[pl.ds(row, 1), :]) + sh_ref[pl.ds(row, 1), :]).astype(BF16)
    proj = _dot(h, win_ref[...])
    ones128 = ones128_ref[...]
    ones64 = ones64_ref[...]

    cq = proj[:, P_CQ:P_CQ + 256]
    cqn = cq * lax.rsqrt(jnp.mean(cq * cq, axis=-1, keepdims=True) + EPS) * gqa_ref[...]
    qa = _head_norm(_dot(cqn.astype(BF16), wuq_ref[...]), ones128, qna_ref[...], 1.0 / A_QK)
    ckv = proj[:, P_CKV:P_CKV + 128]
    ckv_n = ckv * lax.rsqrt(jnp.mean(ckv * ckv, axis=-1, keepdims=True) + EPS) * gkva_ref[...]
    kr_pad = proj[:, P_KR:P_KR + 128]
    ka, va = _mla_keys(ckv_n.astype(BF16), kr_pad, wukv_ref[...], ones128, kna_ref[...])
    if latent:
        cos_a, sin_a = cos_a_ref[...], sin_a_ref[...]
        qa = _rope(qa, cos_a, sin_a, 8)
        ka = _rope(ka, cos_a, sin_a, 8)
    feat_ref[:, QA:QA + 512] = (qa * (A_QK ** -0.5)).astype(BF16)
    feat_ref[:, KA:KA + 512] = ka.astype(BF16)
    feat_ref[:, VA:VA + 256] = va.astype(BF16)

    qb = _head_norm(proj[:, P_QB:P_QB + 256], ones64, qnb_ref[...], 1.0 / HEAD_DIM)
    kb = _head_norm(proj[:, P_KB:P_KB + 128], ones64[:128, :128], knb_ref[:, :128], 1.0 / HEAD_DIM)
    vb = proj[:, P_VB:P_VB + 128]
    if latent:
        cos_b, sin_b = cos_b_ref[...], sin_b_ref[...]
        qb = _rope(qb, cos_b, sin_b, 16)
        kb = _rope(kb, cos_b, sin_b, 16)
    else:
        wk_ref[...] = kb
        wv_ref[...] = vb
    feat_ref[:, QB:QB + 256] = (qb * (HEAD_DIM ** -0.5)).astype(BF16)
    feat_ref[:, KB:KB + 128] = kb.astype(BF16)
    feat_ref[:, VB:VB + 128] = vb.astype(BF16)

    feat_ref[:, XC:XC + 256] = proj[:, P_XC:P_XC + 256].astype(BF16)

    qd = _head_norm(proj[:, P_QD:P_QD + 256], ones64, qnd_ref[...], 1.0 / HEAD_DIM)
    kd = _head_norm(proj[:, P_KD:P_KD + 256], ones64, knd_ref[...], 1.0 / HEAD_DIM)
    vd = proj[:, P_VD:P_VD + 256]
    feat_ref[:, QD:QD + 256] = (qd * (HEAD_DIM ** -0.5)).astype(BF16)
    feat_ref[:, KD:KD + 256] = kd.astype(BF16)
    feat_ref[:, VD:VD + 256] = vd.astype(BF16)

    if not latent:
        ckv_ref[...] = ckv_n
        kr_ref[...] = kr_pad[:, :A_ROPE]
        nk_ref[...] = kd
        nv_ref[...] = vd


_PREP_WEIGHTS = ("g_mix", "w_in", "g_qa", "w_uq", "g_kva", "w_ukv",
                 "qn_a", "kn_a", "qn_b", "kn_b", "qn_d", "kn_d")


def _prep_call(x, mod, layer, lw, consts, *, latent):
    t = x.shape[0]
    tm = 256
    blocks_per_row = DEC_SEQ // tm
    full = lambda a: pl.BlockSpec(a.shape, lambda i: (0,) * a.ndim)
    mod_spec = lambda k: pl.BlockSpec((None, MOD_ROWS, D_MODEL), lambda i: (layer, 0, k))
    lyr = lambda a: pl.BlockSpec((None,) + a.shape[1:], lambda i: (layer,) + (0,) * (a.ndim - 1))
    tok = lambda w: pl.BlockSpec((tm, w), lambda i: (i, 0))
    weights = [lw[k] for k in _PREP_WEIGHTS]
    in_specs = [tok(D_MODEL), mod_spec(3), mod_spec(4)] + [lyr(a) for a in weights]
    in_specs += [full(consts["ones128"]), full(consts["ones64"])]
    args = [x, mod, mod] + weights + [consts["ones128"], consts["ones64"]]
    out_specs = [tok(FEAT)]
    out_shape = [jax.ShapeDtypeStruct((t, FEAT), BF16)]
    if latent:
        pos = pl.BlockSpec((tm, LANES), lambda i: (i % blocks_per_row, 0))
        tabs = [consts["cos_a"], consts["sin_a"], consts["cos_b"], consts["sin_b"]]
        in_specs += [pos] * len(tabs)
        args += tabs
    else:
        for w in (128, A_ROPE, 128, 128, 256, 256):
            out_specs.append(tok(w))
            out_shape.append(jax.ShapeDtypeStruct((t, w), F32))
    return pl.pallas_call(
        functools.partial(_prep_kernel, row0=1 if latent else 0, blocks_per_row=blocks_per_row, latent=latent),
        grid=(t // tm,),
        in_specs=in_specs,
        out_specs=out_specs,
        out_shape=out_shape,
        compiler_params=_cparams(("parallel",)),
        name="prep_lat" if latent else "prep_ctx",
    )(*args)


def _slot_mask(shape, width, slot):
    lane = lax.broadcasted_iota(jnp.int32, shape, 1)
    return (lane >= slot * width) & (lane < (slot + 1) * width)


def _softmax_pv(scores, values, sink=None):
    m = functools.reduce(jnp.maximum, [jnp.max(s, axis=-1, keepdims=True) for s in scores])
    if sink is not None:
        m = jnp.maximum(m, sink)
    ps = [jnp.exp(s - m) for s in scores]
    den = functools.reduce(jnp.add, [jnp.sum(p, axis=-1, keepdims=True) for p in ps])
    if sink is not None:
        den = den + jnp.exp(sink - m)
    o = functools.reduce(jnp.add, [_dot(p.astype(BF16), v) for p, v in zip(ps, values)])
    return o / den


def _attn_ctx_kernel(sink_ref, feat_ref, cl_ref, sl_ref, ccb_ref, scb_ref, o_ref):
    n = feat_ref.shape[0]
    zero = jnp.zeros((), BF16)

    va = feat_ref[:, VA:VA + 256]
    oa = jnp.zeros((n, 256), F32)
    for h in range(N_HEADS):
        q = feat_ref[:, QA + 128 * h:QA + 128 * (h + 1)]
        k = feat_ref[:, KA + 128 * h:KA + 128 * (h + 1)]
        o = _softmax_pv([_dot_t(q, k)], [va])
        oa = jnp.where(_slot_mask(o.shape, HEAD_DIM, h), o, oa)
    o_ref[:, 0:256] = oa.astype(BF16)

    kb = feat_ref[:, KB:KB + 128]
    vb = feat_ref[:, VB:VB + 128]
    for c in range(2):
        qc = feat_ref[:, QB + 128 * c:QB + 128 * (c + 1)]
        ob = jnp.zeros((n, 128), F32)
        for half in range(2):
            head = B_ORDER[2 * c + half]
            msk = _slot_mask(qc.shape, HEAD_DIM, half)
            o = _softmax_pv([_dot_t(jnp.where(msk, qc, zero), kb)], [vb], sink=sink_ref[head])
            ob = jnp.where(msk, o, ob)
        o_ref[:, 256 + 128 * c:256 + 128 * (c + 1)] = ob.astype(BF16)

    xc = feat_ref[:, XC:XC + 256]
    fa = _dot(xc, ccb_ref[...]).astype(BF16)
    fb = _dot(xc, scb_ref[...]).astype(BF16)
    o_ref[:, 512:768] = (_dot(cl_ref[...], fa) - _dot(sl_ref[...], fb)).astype(BF16)

    qd = feat_ref[:, QD:QD + 256]
    kd = feat_ref[:, KD:KD + 256]
    vd = feat_ref[:, VD:VD + 256]
    od = jnp.zeros((n, 256), F32)
    for h in range(N_HEADS):
        msk = _slot_mask(qd.shape, HEAD_DIM, h)
        o = _softmax_pv([_dot_t(jnp.where(msk, qd, zero), kd)], [vd])
        od = jnp.where(msk, o, od)
    o_ref[:, 768:1024] = od.astype(BF16)


def _attn_ctx_call(feat, sink, consts):
    t = feat.shape[0]
    full = lambda a: pl.BlockSpec(a.shape, lambda i: (0,) * a.ndim)
    tabs = [consts["cl_ctx"], consts["sl_ctx"], consts["ccb_ctx"], consts["scb_ctx"]]
    return pl.pallas_call(
        _attn_ctx_kernel,
        grid=(t // SEQ,),
        in_specs=[pl.BlockSpec(memory_space=pltpu.SMEM),
                  pl.BlockSpec((SEQ, FEAT), lambda i: (i, 0))] + [full(a) for a in tabs],
        out_specs=pl.BlockSpec((SEQ, D_MODEL), lambda i: (i, 0)),
        out_shape=jax.ShapeDtypeStruct((t, D_MODEL), BF16),
        compiler_params=_cparams(("parallel",)),
        name="attn_ctx",
    )(sink, feat, *tabs)


def _attn_lat_kernel(sink_ref, feat_ref, cckv_ref, ckr_ref, cwk_ref, cwv_ref, cnk_ref, cnv_ref,
                     wukv_ref, kna_ref, ones128_ref, tt_ref, cl_ref, sl_ref, ccb_ref, scb_ref,
                     o_ref, kac_scr, vac_scr, fa_scr, fb_scr):
    j = pl.program_id(1)
    nq = Q_BLOCK
    zero = jnp.zeros((), BF16)

    @pl.when(j == 0)
    def _():
        ka_c, va_c = _mla_keys(cckv_ref[...].astype(BF16), ckr_ref[...], wukv_ref[...],
                               ones128_ref[...], kna_ref[...])
        kac_scr[...] = ka_c.astype(BF16)
        vac_scr[...] = va_c.astype(BF16)
        xc = feat_ref[:, XC:XC + 256]
        fa_scr[...] = _dot(xc, ccb_ref[...]).astype(BF16)
        fb_scr[...] = _dot(xc, scb_ref[...]).astype(BF16)

    r0 = pl.multiple_of(j * nq, nq)
    rows = pl.ds(r0, nq)

    va = feat_ref[:, VA:VA + 256]
    va_c = vac_scr[...]
    oa = jnp.zeros((nq, 256), F32)
    for h in range(N_HEADS):
        q = feat_ref[rows, QA + 128 * h:QA + 128 * (h + 1)]
        s_c = _dot_t(q, kac_scr[:, 128 * h:128 * (h + 1)])
        s_l = _dot_t(q, feat_ref[:, KA + 128 * h:KA + 128 * (h + 1)])
        o = _softmax_pv([s_c, s_l], [va_c, va])
        oa = jnp.where(_slot_mask(o.shape, HEAD_DIM, h), o, oa)
    o_ref[:, 0:256] = oa.astype(BF16)

    start = pl.multiple_of(jnp.clip(r0 - B_WINDOW, 0, DEC_SEQ - B_KEYS), B_WINDOW)
    kw = feat_ref[pl.ds(start, B_KEYS), KB:KB + 128]
    vw = feat_ref[pl.ds(start, B_KEYS), VB:VB + 128]
    kb_c = cwk_ref[...].astype(BF16)
    vb_c = cwv_ref[...].astype(BF16)
    qpos = r0 + lax.broadcasted_iota(jnp.int32, (nq, B_KEYS), 0)
    kpos = start + lax.broadcasted_iota(jnp.int32, (nq, B_KEYS), 1)
    band = jnp.abs(qpos - kpos) <= B_WINDOW
    for c in range(2):
        qc = feat_ref[rows, QB + 128 * c:QB + 128 * (c + 1)]
        ob = jnp.zeros((nq, 128), F32)
        for half in range(2):
            head = B_ORDER[2 * c + half]
            msk = _slot_mask(qc.shape, HEAD_DIM, half)
            q = jnp.where(msk, qc, zero)
            s_l = jnp.where(band, _dot_t(q, kw), NEG)
            s_c = _dot_t(q, kb_c)
            o = _softmax_pv([s_l, s_c], [vw, vb_c], sink=sink_ref[head])
            ob = jnp.where(msk, o, ob)
        o_ref[:, 256 + 128 * c:256 + 128 * (c + 1)] = ob.astype(BF16)

    o_ref[:, 512:768] = (_dot(cl_ref[rows, :], fa_scr[...]) - _dot(sl_ref[rows, :], fb_scr[...])).astype(BF16)

    rows_per_block = nq // GRID_W
    n_rows = DEC_SEQ // GRID_W
    kr_win = min(NA_ROWS, n_rows)
    ks = jnp.where(j < 2, 0, n_rows - D_KEY_ROWS)
    k0_tok = pl.multiple_of(ks * GRID_W, GRID_W)
    keys = pl.ds(k0_tok, D_KEY_ROWS * GRID_W)
    kdw = feat_ref[keys, KD:KD + 256]
    vdw = feat_ref[keys, VD:VD + 256]
    kd_c = cnk_ref[...].astype(BF16)
    vd_c = cnv_ref[...].astype(BF16)
    qd = feat_ref[rows, QD:QD + 256]
    first = lax.broadcasted_iota(jnp.int32, (GRID_W, LANES), 1) < GRID_W
    od = jnp.zeros((nq, 256), F32)
    for h in range(N_HEADS):
        bias_rows = []
        for rr in range(rows_per_block):
            r = j * rows_per_block + rr
            rs = jnp.clip(r - kr_win // 2, 0, n_rows - kr_win)
            tiles = []
            for p in range(D_KEY_ROWS // 2):
                ka0 = ks + 2 * p
                e = jnp.clip(ka0 - r + NA_ROWS, 0, 2 * NA_ROWS - 1)
                ok0 = ((ka0 >= rs) & (ka0 < rs + kr_win)).astype(jnp.int32)
                ok1 = ((ka0 + 1 >= rs) & (ka0 + 1 < rs + kr_win)).astype(jnp.int32)
                ok = jnp.where(first, ok0, ok1) > 0
                tiles.append(jnp.where(ok, tt_ref[h, e], NEG))
            bias_rows.append(jnp.concatenate(tiles, axis=1))
        bias = jnp.concatenate(bias_rows, axis=0)
        msk = _slot_mask(qd.shape, HEAD_DIM, h)
        q = jnp.where(msk, qd, zero)
        s_l = _dot_t(q, kdw) + bias
        s_c = _dot_t(q, kd_c)
        o = _softmax_pv([s_l, s_c], [vdw, vd_c])
        od = jnp.where(msk, o, od)
    o_ref[:, 768:1024] = od.astype(BF16)


def _attn_lat_call(feat, sink, layer, caches, lw, consts, tt):
    nb = feat.shape[0] // DEC_SEQ
    full = lambda a: pl.BlockSpec(a.shape, lambda b, j: (0,) * a.ndim)
    lyr = lambda a: pl.BlockSpec((None,) + a.shape[1:], lambda b, j: (layer,) + (0,) * (a.ndim - 1))
    cache = lambda a: pl.BlockSpec((None, None) + a.shape[2:], lambda b, j: (b, layer, 0, 0))
    tabs = [consts["cl_lat"], consts["sl_lat"], consts["ccb_lat"], consts["scb_lat"]]
    in_specs = [pl.BlockSpec(memory_space=pltpu.SMEM),
                pl.BlockSpec((DEC_SEQ, FEAT), lambda b, j: (b, 0))]
    in_specs += [cache(a) for a in caches]
    in_specs += [lyr(lw["w_ukv"]), lyr(lw["kn_a"]), full(consts["ones128"]),
                 pl.BlockSpec((N_HEADS,) + tt.shape[1:], lambda b, j: (layer, 0, 0, 0))]
    in_specs += [full(a) for a in tabs]
    return pl.pallas_call(
        _attn_lat_kernel,
        grid=(nb, DEC_SEQ // Q_BLOCK),
        in_specs=in_specs,
        out_specs=pl.BlockSpec((Q_BLOCK, D_MODEL), lambda b, j: (b * (DEC_SEQ // Q_BLOCK) + j, 0)),
        out_shape=jax.ShapeDtypeStruct((feat.shape[0], D_MODEL), BF16),
        scratch_shapes=[pltpu.VMEM((PAST_LEN, 512), BF16), pltpu.VMEM((PAST_LEN, 256), BF16),
                        pltpu.VMEM((DEC_SEQ, 256), BF16), pltpu.VMEM((DEC_SEQ, 256), BF16)],
        compiler_params=_cparams(("parallel", "arbitrary")),
        name="attn_lat",
    )(sink, feat, *caches, lw["w_ukv"], lw["kn_a"], consts["ones128"], tt, *tabs)


def _layer_weights(w_in, w_uq, w_ukv, w_o, g_mix, g_qa, g_kva, qn_a, kn_a, qn_b, kn_b, qn_d, kn_d):
    z = lambda *s: jnp.zeros(s, F32)
    off = 416
    w_in_p = jnp.concatenate(
        [w_in[..., :384], w_in[..., 384:416], z(DEPTH, D_MODEL, LANES - A_ROPE)]
        + [w_in[..., off + HEAD_DIM * h:off + HEAD_DIM * (h + 1)] for h in B_ORDER]
        + [w_in[..., 672:]], axis=-1).astype(BF16)

    def slots(a, rows):
        parts = []
        for h in range(N_HEADS):
            hd = a[..., A_QK * h:A_QK * (h + 1)]
            parts += [hd[..., HEAD_DIM:], z(*rows, LANES - A_QK), hd[..., :HEAD_DIM]]
        return jnp.concatenate(parts, axis=-1)

    w_uq_p = slots(w_uq, (DEPTH, 256)).astype(BF16)
    k_cols, v_cols = [], []
    for h in range(N_HEADS):
        k_cols += [z(DEPTH, 128, HEAD_DIM), w_ukv[..., 128 * h:128 * h + HEAD_DIM]]
        v_cols.append(w_ukv[..., 128 * h + HEAD_DIM:128 * (h + 1)])
    w_ukv_p = jnp.concatenate(k_cols + v_cols, axis=-1).astype(BF16)
    w_o_p = jnp.concatenate(
        [w_o[:, :256]] + [w_o[:, 256 + HEAD_DIM * h:256 + HEAD_DIM * (h + 1)] for h in B_ORDER]
        + [w_o[:, 512:]], axis=1).astype(BF16)
    gain_a = lambda g: slots(jnp.tile(g, (1, N_HEADS)), (DEPTH,))[:, None, :]
    tile4 = lambda g: jnp.tile(g, (1, N_HEADS))[:, None, :]
    lw = dict(w_in=w_in_p, w_uq=w_uq_p, w_ukv=w_ukv_p,
              g_mix=g_mix[:, None, :], g_qa=g_qa[:, None, :], g_kva=g_kva[:, None, :],
              qn_a=gain_a(qn_a), kn_a=gain_a(kn_a), qn_b=tile4(qn_b), kn_b=tile4(kn_b),
              qn_d=tile4(qn_d), kn_d=tile4(kn_d))
    return lw, w_o_p


def _constants():
    consts = {}
    consts["cos_a"], consts["sin_a"] = _rope_tables(A_ROPE, LANES, DEC_SEQ)
    consts["cos_b"], consts["sin_b"] = _rope_tables(HEAD_DIM, HEAD_DIM, DEC_SEQ)
    for tag, n in (("ctx", SEQ), ("lat", DEC_SEQ)):
        cl, sl, ccb, scb = _fourier_constants(n)
        consts["cl_" + tag], consts["sl_" + tag], consts["ccb_" + tag], consts["scb_" + tag] = cl, sl, ccb, scb
    consts["ones128"] = _group_ones(512, LANES)
    consts["ones64"] = _group_ones(256, HEAD_DIM)
    return consts


def kernel(x_prompt, x_sample, cache_mla_ckv, cache_mla_krope, cache_win_k, cache_win_v, cache_na_k, cache_na_v, c, c_ctx, w_ada, b_ada, g_ffn1, w_gate1, w_up1, w_down1, g_mix, w_in, g_qa, w_uq, g_kva, w_ukv, qn_a, kn_a, qn_b, kn_b, sink_b, qn_d, kn_d, rpb_d, w_o, g_ffn2, w_gate2, w_up2, w_down2):
    batch, seq, _ = x_prompt.shape
    dec_batch, dec_seq, _ = x_sample.shape
    assert (seq, dec_seq, dec_batch + 1 <= MOD_ROWS) == (SEQ, DEC_SEQ, True)

    consts = _constants()
    lw, w_o_p = _layer_weights(w_in, w_uq, w_ukv, w_o, g_mix, g_qa, g_kva, qn_a, kn_a, qn_b, kn_b, qn_d, kn_d)

    cond = jnp.concatenate([c_ctx[None, :], c, jnp.zeros((MOD_ROWS - 1 - dec_batch, D_MODEL), F32)], axis=0)
    mod = _ada_call(cond, w_ada, b_ada)
    tt = _rpb_call(rpb_d)

    caches = [cache_mla_ckv,
              jnp.pad(cache_mla_krope, ((0, 0), (0, 0), (0, 0), (0, LANES - A_ROPE))),
              cache_win_k.reshape(dec_batch, DEPTH, PAST_LEN, 128),
              cache_win_v.reshape(dec_batch, DEPTH, PAST_LEN, 128),
              cache_na_k.reshape(dec_batch, DEPTH, PAST_LEN, 256),
              cache_na_v.reshape(dec_batch, DEPTH, PAST_LEN, 256)]

    xp = x_prompt.reshape(batch * seq, D_MODEL)
    xs = x_sample.reshape(dec_batch * dec_seq, D_MODEL)
    new = [[] for _ in range(6)]
    for l in range(DEPTH):
        sink = sink_b[l]
        xp = _ffn_call(xp, mod, l, 0, g_ffn1, w_gate1, w_up1, w_down1, latent=False)
        feat, *aux = _prep_call(xp, mod, l, lw, consts, latent=False)
        for dst, a in zip(new, aux):
            dst.append(a)
        mix = _attn_ctx_call(feat, sink, consts)
        xp = _ffn_call(xp, mod, l, 2, g_ffn2, w_gate2, w_up2, w_down2, latent=False, mix=mix, w_o=w_o_p)

        xs = _ffn_call(xs, mod, l, 0, g_ffn1, w_gate1, w_up1, w_down1, latent=True)
        (feat,) = _prep_call(xs, mod, l, lw, consts, latent=True)
        mix = _attn_lat_call(feat, sink, l, caches, lw, consts, tt)
        xs = _ffn_call(xs, mod, l, 2, g_ffn2, w_gate2, w_up2, w_down2, latent=True, mix=mix, w_o=w_o_p)

    stack = lambda parts, tail: jnp.stack([a.reshape((batch, seq) + tail) for a in parts], axis=1)
    return (xp.reshape(batch, seq, D_MODEL), xs.reshape(dec_batch, dec_seq, D_MODEL),
            stack(new[0], (128,)), stack(new[1], (A_ROPE,)),
            stack(new[2], (2, HEAD_DIM)), stack(new[3], (2, HEAD_DIM)),
            stack(new[4], (N_HEADS, HEAD_DIM)), stack(new[5], (N_HEADS, HEAD_DIM)))
```

```python
import functools

import numpy as np
import jax
import jax.numpy as jnp
from jax import lax
from jax.experimental import pallas as pl
from jax.experimental.pallas import tpu as pltpu

F32 = jnp.float32
BF16 = jnp.bfloat16

D_MODEL = 1024
DEPTH = 4
SEQ = 256
DEC_SEQ = 1024
PAST_LEN = 256
GRID_W = 64
D_FF = 2816
N_MOD = 9
EPS = 1e-6
ROPE_BASE = 10000.0
A_QK = 96
A_ROPE = 32
B_WINDOW = 128
NA_ROWS = 8
NA_COLS = 16
HEAD_DIM = 64
N_HEADS = 4
NEG = -1e30

LANES = 128
MOD_ROWS = 8
VMEM_LIMIT = 56 * 1024 * 1024

QA, KA, VA = 0, 512, 1024
QB, KB, VB = 1280, 1536, 1664
XC = 1792
QD, KD, VD = 2048, 2304, 2560
FEAT = 2816
B_ORDER = (0, 2, 1, 3)

P_CQ, P_CKV, P_KR, P_QB, P_KB, P_VB, P_XC, P_QD, P_KD, P_VD = (
    0, 256, 384, 512, 768, 896, 1024, 1280, 1536, 1792)
P_WIDTH = 2048

TOK_BLOCK = 512
PREP_BLOCK = 256
FF_CHUNK = 256
N_FF_CHUNKS = D_FF // FF_CHUNK
Q_BLOCK = 256
B_KEYS = 512
D_KEY_ROWS = 12


def _cparams(sem):
    return pltpu.CompilerParams(dimension_semantics=sem, vmem_limit_bytes=VMEM_LIMIT)


def _dot(a, b):
    return jnp.dot(a, b, preferred_element_type=F32)


def _dot_t(a, b):
    return lax.dot_general(a, b, (((1,), (1,)), ((), ())), preferred_element_type=F32)


def _group_sums(sq, ones_bf16):
    hi = sq.astype(BF16)
    lo = (sq - hi.astype(F32)).astype(BF16)
    return _dot(hi, ones_bf16) + _dot(lo, ones_bf16)


def _rope_tables(width, slot, n_pos):
    half = width // 2
    quarter = half // 2
    inv = (np.float32(ROPE_BASE) ** (-np.arange(0, half, 2, dtype=np.float32) / np.float32(half))).astype(np.float32)
    t = np.arange(n_pos)
    pos = np.stack([t // GRID_W, t % GRID_W], axis=0).astype(np.float32)
    cos = np.ones((n_pos, LANES), np.float32)
    sin = np.zeros((n_pos, LANES), np.float32)
    for lane in range(LANES):
        i = lane % slot
        if i >= width:
            continue
        axis = i // half
        j = i % half
        ang = pos[axis] * inv[j % quarter]
        cos[:, lane] = np.cos(ang)
        sin[:, lane] = np.sin(ang) * (-1.0 if j < quarter else 1.0)
    return jnp.asarray(cos), jnp.asarray(sin)


def _dft_tables(n):
    k = np.arange(n)
    ang = 2.0 * np.pi * ((k[:, None] * k[None, :]) % n) / n
    return np.cos(ang), np.sin(ang)


def _fourier_constants(n_pos):
    cl, sl = _dft_tables(n_pos)
    cc, sc = _dft_tables(HEAD_DIM)
    scale = 1.0 / np.sqrt(float(HEAD_DIM * n_pos))
    eye = np.eye(N_HEADS)
    ccb = np.kron(eye, cc) * scale
    scb = np.kron(eye, sc) * scale
    as_bf = lambda a: jnp.asarray(a.astype(np.float32)).astype(BF16)
    return as_bf(cl), as_bf(sl), as_bf(ccb), as_bf(scb)


def _group_ones(width, group):
    g = np.arange(width) // group
    return jnp.asarray((g[:, None] == g[None, :]).astype(np.float32)).astype(BF16)


def _ada_kernel(cond_ref, w_ref, b_ref, o_ref):
    c = cond_ref[...]
    s = (c * jax.nn.sigmoid(c)).astype(BF16)
    o_ref[...] = _dot(s, w_ref[...].astype(BF16)) + b_ref[...]


def _ada_call(cond, w_ada, b_ada):
    tn = 1024
    n = N_MOD * D_MODEL
    return pl.pallas_call(
        _ada_kernel,
        grid=(DEPTH, n // tn),
        in_specs=[
            pl.BlockSpec((MOD_ROWS, D_MODEL), lambda l, j: (0, 0)),
            pl.BlockSpec((None, D_MODEL, tn), lambda l, j: (l, 0, j)),
            pl.BlockSpec((None, 1, tn), lambda l, j: (l, 0, j)),
        ],
        out_specs=pl.BlockSpec((None, MOD_ROWS, tn), lambda l, j: (l, 0, j)),
        out_shape=jax.ShapeDtypeStruct((DEPTH, MOD_ROWS, n), F32),
        compiler_params=_cparams(("parallel", "parallel")),
        name="ada_mod",
    )(cond, w_ada, b_ada.reshape(DEPTH, 1, n))


def _rpb_kernel(rpb_ref, o_ref):
    g = pl.program_id(0)
    n_dr = 2 * NA_ROWS - 1
    n_dc = 2 * NA_COLS - 1
    base = g * (n_dr * n_dc)
    shape = (GRID_W, LANES)
    q = lax.broadcasted_iota(jnp.int32, shape, 0)
    lane = lax.broadcasted_iota(jnp.int32, shape, 1)
    w = lane & (GRID_W - 1)
    dc = jnp.clip(w - q + (NA_COLS - 1), 0, n_dc - 1)
    cs = jnp.clip(q - NA_COLS // 2, 0, GRID_W - NA_COLS)
    col_ok = (w >= cs) & (w < cs + NA_COLS)
    neg = jnp.full(shape, NEG, F32)
    tiles = []
    for d in range(n_dr):
        acc = jnp.zeros(shape, F32)
        for j in range(n_dc):
            acc = jnp.where(dc == j, rpb_ref[base + d * n_dc + j], acc)
        tiles.append(jnp.where(col_ok, acc, neg))
    first = lane < GRID_W
    for e in range(n_dr + 1):
        left = tiles[e - 1] if e >= 1 else neg
        right = tiles[e] if e < n_dr else neg
        o_ref[e] = jnp.where(first, left, right)


def _rpb_call(rpb_d):
    n_e = 2 * NA_ROWS
    return pl.pallas_call(
        _rpb_kernel,
        grid=(DEPTH * N_HEADS,),
        in_specs=[pl.BlockSpec(memory_space=pltpu.SMEM)],
        out_specs=pl.BlockSpec((None, n_e, GRID_W, LANES), lambda g: (g, 0, 0, 0)),
        out_shape=jax.ShapeDtypeStruct((DEPTH * N_HEADS, n_e, GRID_W, LANES), F32),
        compiler_params=_cparams(("parallel",)),
        name="rpb_tiles",
    )(rpb_d.reshape(-1))


def _mod_row(b, n_prompt_blocks, blocks_per_batch):
    return jnp.where(b < n_prompt_blocks, 0, 1 + (b - n_prompt_blocks) // blocks_per_batch)


def _ffn_kernel(*refs, layer, n_prompt_blocks, blocks_per_batch, with_mix):
    if with_mix:
        x_ref, mixp_ref, mixs_ref, wo_ref, g2_ref = refs[:5]
        refs = refs[5:]
    else:
        x_ref = refs[0]
        refs = refs[1:]
    (sh_ref, sc_ref, g_ref, gn_ref, wg_hbm, wu_hbm, wd_hbm, o_ref,
     wg_scr, wu_scr, wd_scr, stg_g, stg_u, stg_d, sem, x_scr, h_scr, a_scr) = refs
    b = pl.program_id(0)
    row = _mod_row(b, n_prompt_blocks, blocks_per_batch)

    x = x_ref[...]
    if with_mix:
        mix = jnp.where(b < n_prompt_blocks, mixp_ref[...], mixs_ref[...])
        x = x + g2_ref[pl.ds(row, 1), :] * _dot(mix, wo_ref[...])
        x_scr[...] = x
    y = x * lax.rsqrt(jnp.mean(x * x, axis=-1, keepdims=True) + EPS) * gn_ref[...]
    h = y * (1.0 + sc_ref[pl.ds(row, 1), :]) + sh_ref[pl.ds(row, 1), :]
    h_scr[...] = h.astype(BF16)

    def weight_copies(c, slot):
        cols = pl.ds(c * FF_CHUNK, FF_CHUNK)
        return (pltpu.make_async_copy(wg_hbm.at[layer, :, cols], stg_g.at[slot], sem.at[0, slot]),
                pltpu.make_async_copy(wu_hbm.at[layer, :, cols], stg_u.at[slot], sem.at[1, slot]),
                pltpu.make_async_copy(wd_hbm.at[layer, cols, :], stg_d.at[slot], sem.at[2, slot]))

    def chunk(c):
        lanes = slice(c * FF_CHUNK, (c + 1) * FF_CHUNK)
        hb = h_scr[...]
        gate = _dot(hb, wg_scr[:, lanes])
        up = _dot(hb, wu_scr[:, lanes])
        a_scr[:, lanes] = ((gate * jax.nn.sigmoid(gate)) * up).astype(BF16)

    @pl.when(b == 0)
    def _():
        for slot in range(2):
            for cp in weight_copies(slot, slot):
                cp.start()
        for c in range(N_FF_CHUNKS):
            slot = c % 2
            lanes = slice(c * FF_CHUNK, (c + 1) * FF_CHUNK)
            for cp in weight_copies(c, slot):
                cp.wait()
            wg_scr[:, lanes] = stg_g[slot].astype(BF16)
            wu_scr[:, lanes] = stg_u[slot].astype(BF16)
            wd_scr[lanes, :] = stg_d[slot].astype(BF16)
            if c + 2 < N_FF_CHUNKS:
                for cp in weight_copies(c + 2, slot):
                    cp.start()
            chunk(c)

    @pl.when(b != 0)
    def _():
        for c in range(N_FF_CHUNKS):
            chunk(c)

    x_in = x_scr[...] if with_mix else x_ref[...]
    o_ref[...] = x_in + (0.5 * g_ref[pl.ds(row, 1), :]) * _dot(a_scr[...], wd_scr[...])


def _ffn_call(x, mod, layer, slot, g_norm, w_gate, w_up, w_down, *, n_prompt, mix=None, w_o=None):
    t = x.shape[0]
    tm = TOK_BLOCK
    npb = n_prompt // tm
    with_mix = mix is not None
    mod_spec = lambda k: pl.BlockSpec((None, MOD_ROWS, D_MODEL), lambda b: (layer, 0, k))
    x_spec = pl.BlockSpec((tm, D_MODEL), lambda b: (b, 0))
    hbm = pl.BlockSpec(memory_space=pl.ANY)
    in_specs = [x_spec]
    args = [x]
    if with_mix:
        in_specs += [pl.BlockSpec((tm, D_MODEL), lambda b: (jnp.minimum(b, npb - 1), 0)),
                     pl.BlockSpec((tm, D_MODEL), lambda b: (jnp.maximum(b - npb, 0), 0)),
                     pl.BlockSpec((None, D_MODEL, D_MODEL), lambda b: (layer, 0, 0)), mod_spec(5)]
        args += [mix[0], mix[1], w_o, mod]
    c0 = 3 * slot
    in_specs += [mod_spec(c0), mod_spec(c0 + 1), mod_spec(c0 + 2),
                 pl.BlockSpec((None, 1, D_MODEL), lambda b: (layer, 0, 0)), hbm, hbm, hbm]
    args += [mod, mod, mod, g_norm.reshape(DEPTH, 1, D_MODEL), w_gate, w_up, w_down]
    scratch = [pltpu.VMEM((D_MODEL, D_FF), BF16), pltpu.VMEM((D_MODEL, D_FF), BF16), pltpu.VMEM((D_FF, D_MODEL), BF16),
               pltpu.VMEM((2, D_MODEL, FF_CHUNK), F32), pltpu.VMEM((2, D_MODEL, FF_CHUNK), F32),
               pltpu.VMEM((2, FF_CHUNK, D_MODEL), F32), pltpu.SemaphoreType.DMA((3, 2)),
               pltpu.VMEM((tm, D_MODEL), F32), pltpu.VMEM((tm, D_MODEL), BF16), pltpu.VMEM((tm, D_FF), BF16)]
    return pl.pallas_call(
        functools.partial(_ffn_kernel, layer=layer, n_prompt_blocks=npb, blocks_per_batch=DEC_SEQ // tm,
                          with_mix=with_mix),
        grid=(t // tm,),
        in_specs=in_specs,
        out_specs=x_spec,
        out_shape=jax.ShapeDtypeStruct((t, D_MODEL), F32),
        scratch_shapes=scratch,
        compiler_params=_cparams(("arbitrary",)),
        name="ffn_mix" if with_mix else "ffn",
    )(*args)


def _swap(x, width):
    lane = lax.broadcasted_iota(jnp.int32, x.shape, 1)
    first = (lane & width) == 0
    return jnp.where(first, pltpu.roll(x, LANES - width, 1), pltpu.roll(x, width, 1))


def _rope(x, cos, sin, width):
    parts = []
    for c in range(x.shape[1] // LANES):
        xc = x[:, c * LANES:(c + 1) * LANES]
        parts.append(xc * cos + _swap(xc, width) * sin)
    return parts[0] if len(parts) == 1 else jnp.concatenate(parts, axis=1)


def _head_norm(x, ones, gain, inv_dim):
    ms = _group_sums(x * x, ones) * inv_dim
    return x * lax.rsqrt(ms + EPS) * gain


def _mla_keys(ckv_n_bf16, kr_pad, w_ukv, ones128, kn_a):
    kv = _dot(ckv_n_bf16, w_ukv)
    k = kv[:, :512] + jnp.concatenate([kr_pad] * N_HEADS, axis=1)
    return _head_norm(k, ones128, kn_a, 1.0 / A_QK), kv[:, 512:]


def _prep_kernel(*refs, n_prompt_blocks, blocks_per_batch, n_alias):
    (x_ref, sh_ref, sc_ref, gn_ref, win_ref, gqa_ref, wuq_ref, gkva_ref, wukv_ref,
     qna_ref, kna_ref, qnb_ref, knb_ref, qnd_ref, knd_ref,
     ones128_ref, ones64_ref, cos_a_ref, sin_a_ref, cos_b_ref, sin_b_ref) = refs[:21]
    feat_ref, ckv_ref, kr_ref, wk_ref, wv_ref, nk_ref, nv_ref = refs[21 + n_alias:]
    b = pl.program_id(0)
    row = _mod_row(b, n_prompt_blocks, blocks_per_batch)
    is_prompt = b < n_prompt_blocks

    x = x_ref[...]
    y = x * lax.rsqrt(jnp.mean(x * x, axis=-1, keepdims=True) + EPS) * gn_ref[...]
    h = (y * (1.0 + sc_ref[pl.ds(row, 1), :]) + sh_ref[pl.ds(row, 1), :]).astype(BF16)
    proj = _dot(h, win_ref[...])
    ones128 = ones128_ref[...]
    ones64 = ones64_ref[...]

    cq = proj[:, P_CQ:P_CQ + 256]
    cqn = cq * lax.rsqrt(jnp.mean(cq * cq, axis=-1, keepdims=True) + EPS) * gqa_ref[...]
    qa = _head_norm(_dot(cqn.astype(BF16), wuq_ref[...]), ones128, qna_ref[...], 1.0 / A_QK)
    ckv = proj[:, P_CKV:P_CKV + 128]
    ckv_n = ckv * lax.rsqrt(jnp.mean(ckv * ckv, axis=-1, keepdims=True) + EPS) * gkva_ref[...]
    kr_pad = proj[:, P_KR:P_KR + 128]
    ka, va = _mla_keys(ckv_n.astype(BF16), kr_pad, wukv_ref[...], ones128, kna_ref[...])
    cos_a, sin_a = cos_a_ref[...], sin_a_ref[...]
    qa = _rope(qa, cos_a, sin_a, 8)
    ka = _rope(ka, cos_a, sin_a, 8)
    feat_ref[:, QA:QA + 512] = (qa * (A_QK ** -0.5)).astype(BF16)
    feat_ref[:, KA:KA + 512] = ka.astype(BF16)
    feat_ref[:, VA:VA + 256] = va.astype(BF16)

    qb = _head_norm(proj[:, P_QB:P_QB + 256], ones64, qnb_ref[...], 1.0 / HEAD_DIM)
    kb = _head_norm(proj[:, P_KB:P_KB + 128], ones64[:128, :128], knb_ref[:, :128], 1.0 / HEAD_DIM)
    vb = proj[:, P_VB:P_VB + 128]
    cos_b, sin_b = cos_b_ref[...], sin_b_ref[...]
    qb = _rope(qb, cos_b, sin_b, 16)
    kb = _rope(kb, cos_b, sin_b, 16)
    feat_ref[:, QB:QB + 256] = (qb * (HEAD_DIM ** -0.5)).astype(BF16)
    feat_ref[:, KB:KB + 128] = kb.astype(BF16)
    feat_ref[:, VB:VB + 128] = vb.astype(BF16)

    feat_ref[:, XC:XC + 256] = proj[:, P_XC:P_XC + 256].astype(BF16)

    qd = _head_norm(proj[:, P_QD:P_QD + 256], ones64, qnd_ref[...], 1.0 / HEAD_DIM)
    kd = _head_norm(proj[:, P_KD:P_KD + 256], ones64, knd_ref[...], 1.0 / HEAD_DIM)
    vd = proj[:, P_VD:P_VD + 256]
    feat_ref[:, QD:QD + 256] = (qd * (HEAD_DIM ** -0.5)).astype(BF16)
    feat_ref[:, KD:KD + 256] = kd.astype(BF16)
    feat_ref[:, VD:VD + 256] = vd.astype(BF16)

    @pl.when(is_prompt)
    def _():
        ckv_ref[...] = ckv_n
        kr_ref[...] = kr_pad[:, :A_ROPE]
        wk_ref[...] = kb
        wv_ref[...] = vb
        nk_ref[...] = kd
        nv_ref[...] = vd


_PREP_WEIGHTS = ("g_mix", "w_in", "g_qa", "w_uq", "g_kva", "w_ukv",
                 "qn_a", "kn_a", "qn_b", "kn_b", "qn_d", "kn_d")
_CACHE_WIDTHS = (128, A_ROPE, 128, 128, 256, 256)


def _prep_call(x, mod, layer, lw, consts, new_caches, *, n_prompt):
    t = x.shape[0]
    tm = PREP_BLOCK
    npb = n_prompt // tm
    bpb = DEC_SEQ // tm
    batch = n_prompt // SEQ
    full = lambda a: pl.BlockSpec(a.shape, lambda b: (0,) * a.ndim)
    mod_spec = lambda k: pl.BlockSpec((None, MOD_ROWS, D_MODEL), lambda b: (layer, 0, k))
    lyr = lambda a: pl.BlockSpec((None,) + a.shape[1:], lambda b: (layer,) + (0,) * (a.ndim - 1))
    tok = lambda w: pl.BlockSpec((tm, w), lambda b: (b, 0))
    pos = pl.BlockSpec((tm, LANES), lambda b: (jnp.where(b < npb, 0, 1 + (b - npb) % bpb), 0))
    weights = [lw[k] for k in _PREP_WEIGHTS]
    tabs = [consts["cos_a"], consts["sin_a"], consts["cos_b"], consts["sin_b"]]
    in_specs = [tok(D_MODEL), mod_spec(3), mod_spec(4)] + [lyr(a) for a in weights]
    in_specs += [full(consts["ones128"]), full(consts["ones64"])] + [pos] * len(tabs)
    args = [x, mod, mod] + weights + [consts["ones128"], consts["ones64"]] + tabs
    aliases = {len(args) + k: 1 + k for k in range(len(new_caches))}
    in_specs += [pl.BlockSpec(memory_space=pl.ANY)] * len(new_caches)
    args += list(new_caches)
    spb = SEQ // tm
    cache_spec = lambda w: pl.BlockSpec(
        (None, None, tm, w), lambda b: (jnp.minimum(b, npb - 1) // spb, layer, jnp.minimum(b, npb - 1) % spb, 0))
    out_specs = [tok(FEAT)] + [cache_spec(w) for w in _CACHE_WIDTHS]
    out_shape = [jax.ShapeDtypeStruct((t, FEAT), BF16)]
    out_shape += [jax.ShapeDtypeStruct((batch, DEPTH, SEQ, w), F32) for w in _CACHE_WIDTHS]
    feat, *caches = pl.pallas_call(
        functools.partial(_prep_kernel, n_prompt_blocks=npb, blocks_per_batch=bpb, n_alias=len(aliases)),
        grid=(t // tm,),
        in_specs=in_specs,
        out_specs=out_specs,
        out_shape=out_shape,
        input_output_aliases=aliases,
        compiler_params=_cparams(("arbitrary",)),
        name="prep",
    )(*args)
    return feat, caches


def _slot_mask(shape, width, slot):
    lane = lax.broadcasted_iota(jnp.int32, shape, 1)
    return (lane >= slot * width) & (lane < (slot + 1) * width)


def _softmax_pv(scores, values, sink=None):
    m = functools.reduce(jnp.maximum, [jnp.max(s, axis=-1, keepdims=True) for s in scores])
    if sink is not None:
        m = jnp.maximum(m, sink)
    ps = [jnp.exp(s - m) for s in scores]
    den = functools.reduce(jnp.add, [jnp.sum(p, axis=-1, keepdims=True) for p in ps])
    if sink is not None:
        den = den + jnp.exp(sink - m)
    o = functools.reduce(jnp.add, [_dot(p.astype(BF16), v) for p, v in zip(ps, values)])
    return o / den


def _attn_ctx_kernel(sink_ref, feat_ref, cl_ref, sl_ref, ccb_ref, scb_ref, o_ref):
    n = feat_ref.shape[0]
    zero = jnp.zeros((), BF16)

    va = feat_ref[:, VA:VA + 256]
    oa = jnp.zeros((n, 256), F32)
    for h in range(N_HEADS):
        q = feat_ref[:, QA + 128 * h:QA + 128 * (h + 1)]
        k = feat_ref[:, KA + 128 * h:KA + 128 * (h + 1)]
        o = _softmax_pv([_dot_t(q, k)], [va])
        oa = jnp.where(_slot_mask(o.shape, HEAD_DIM, h), o, oa)
    o_ref[:, 0:256] = oa.astype(BF16)

    kb = feat_ref[:, KB:KB + 128]
    vb = feat_ref[:, VB:VB + 128]
    for c in range(2):
        qc = feat_ref[:, QB + 128 * c:QB + 128 * (c + 1)]
        ob = jnp.zeros((n, 128), F32)
        for half in range(2):
            head = B_ORDER[2 * c + half]
            msk = _slot_mask(qc.shape, HEAD_DIM, half)
            o = _softmax_pv([_dot_t(jnp.where(msk, qc, zero), kb)], [vb], sink=sink_ref[head])
            ob = jnp.where(msk, o, ob)
        o_ref[:, 256 + 128 * c:256 + 128 * (c + 1)] = ob.astype(BF16)

    xc = feat_ref[:, XC:XC + 256]
    fa = _dot(xc, ccb_ref[...]).astype(BF16)
    fb = _dot(xc, scb_ref[...]).astype(BF16)
    o_ref[:, 512:768] = (_dot(cl_ref[...], fa) - _dot(sl_ref[...], fb)).astype(BF16)

    qd = feat_ref[:, QD:QD + 256]
    kd = feat_ref[:, KD:KD + 256]
    vd = feat_ref[:, VD:VD + 256]
    od = jnp.zeros((n, 256), F32)
    for h in range(N_HEADS):
        msk = _slot_mask(qd.shape, HEAD_DIM, h)
        o = _softmax_pv([_dot_t(jnp.where(msk, qd, zero), kd)], [vd])
        od = jnp.where(msk, o, od)
    o_ref[:, 768:1024] = od.astype(BF16)


def _attn_ctx_call(feat, sink, consts, *, n_prompt):
    t = n_prompt
    full = lambda a: pl.BlockSpec(a.shape, lambda i: (0,) * a.ndim)
    tabs = [consts["cl_ctx"], consts["sl_ctx"], consts["ccb_ctx"], consts["scb_ctx"]]
    return pl.pallas_call(
        _attn_ctx_kernel,
        grid=(t // SEQ,),
        in_specs=[pl.BlockSpec(memory_space=pltpu.SMEM),
                  pl.BlockSpec((SEQ, FEAT), lambda i: (i, 0))] + [full(a) for a in tabs],
        out_specs=pl.BlockSpec((SEQ, D_MODEL), lambda i: (i, 0)),
        out_shape=jax.ShapeDtypeStruct((t, D_MODEL), BF16),
        compiler_params=_cparams(("parallel",)),
        name="attn_ctx",
    )(sink, feat, *tabs)


def _attn_lat_kernel(sink_ref, feat_ref, cckv_ref, ckr_ref, cwk_ref, cwv_ref, cnk_ref, cnv_ref,
                     wukv_ref, kna_ref, ones128_ref, tt_ref, cl_ref, sl_ref, ccb_ref, scb_ref,
                     o_ref, kac_scr, vac_scr, fa_scr, fb_scr):
    j = pl.program_id(1)
    nq = Q_BLOCK
    zero = jnp.zeros((), BF16)

    @pl.when(j == 0)
    def _():
        ka_c, va_c = _mla_keys(cckv_ref[...].astype(BF16), ckr_ref[...], wukv_ref[...],
                               ones128_ref[...], kna_ref[...])
        kac_scr[...] = ka_c.astype(BF16)
        vac_scr[...] = va_c.astype(BF16)
        xc = feat_ref[:, XC:XC + 256]
        fa_scr[...] = _dot(xc, ccb_ref[...]).astype(BF16)
        fb_scr[...] = _dot(xc, scb_ref[...]).astype(BF16)

    r0 = pl.multiple_of(j * nq, nq)
    rows = pl.ds(r0, nq)

    va = feat_ref[:, VA:VA + 256]
    va_c = vac_scr[...]
    oa = jnp.zeros((nq, 256), F32)
    for h in range(N_HEADS):
        q = feat_ref[rows, QA + 128 * h:QA + 128 * (h + 1)]
        s_c = _dot_t(q, kac_scr[:, 128 * h:128 * (h + 1)])
        s_l = _dot_t(q, feat_ref[:, KA + 128 * h:KA + 128 * (h + 1)])
        o = _softmax_pv([s_c, s_l], [va_c, va])
        oa = jnp.where(_slot_mask(o.shape, HEAD_DIM, h), o, oa)
    o_ref[:, 0:256] = oa.astype(BF16)

    start = pl.multiple_of(jnp.clip(r0 - B_WINDOW, 0, DEC_SEQ - B_KEYS), B_WINDOW)
    kw = feat_ref[pl.ds(start, B_KEYS), KB:KB + 128]
    vw = feat_ref[pl.ds(start, B_KEYS), VB:VB + 128]
    kb_c = cwk_ref[...].astype(BF16)
    vb_c = cwv_ref[...].astype(BF16)
    qpos = r0 + lax.broadcasted_iota(jnp.int32, (nq, B_KEYS), 0)
    kpos = start + lax.broadcasted_iota(jnp.int32, (nq, B_KEYS), 1)
    band = jnp.abs(qpos - kpos) <= B_WINDOW
    for c in range(2):
        qc = feat_ref[rows, QB + 128 * c:QB + 128 * (c + 1)]
        ob = jnp.zeros((nq, 128), F32)
        for half in range(2):
            head = B_ORDER[2 * c + half]
            msk = _slot_mask(qc.shape, HEAD_DIM, half)
            q = jnp.where(msk, qc, zero)
            s_l = jnp.where(band, _dot_t(q, kw), NEG)
            s_c = _dot_t(q, kb_c)
            o = _softmax_pv([s_l, s_c], [vw, vb_c], sink=sink_ref[head])
            ob = jnp.where(msk, o, ob)
        o_ref[:, 256 + 128 * c:256 + 128 * (c + 1)] = ob.astype(BF16)

    o_ref[:, 512:768] = (_dot(cl_ref[rows, :], fa_scr[...]) - _dot(sl_ref[rows, :], fb_scr[...])).astype(BF16)

    rows_per_block = nq // GRID_W
    n_rows = DEC_SEQ // GRID_W
    kr_win = min(NA_ROWS, n_rows)
    ks = jnp.where(j < 2, 0, n_rows - D_KEY_ROWS)
    k0_tok = pl.multiple_of(ks * GRID_W, GRID_W)
    keys = pl.ds(k0_tok, D_KEY_ROWS * GRID_W)
    kdw = feat_ref[keys, KD:KD + 256]
    vdw = feat_ref[keys, VD:VD + 256]
    kd_c = cnk_ref[...].astype(BF16)
    vd_c = cnv_ref[...].astype(BF16)
    qd = feat_ref[rows, QD:QD + 256]
    first = lax.broadcasted_iota(jnp.int32, (GRID_W, LANES), 1) < GRID_W
    od = jnp.zeros((nq, 256), F32)
    for h in range(N_HEADS):
        bias_rows = []
        for rr in range(rows_per_block):
            r = j * rows_per_block + rr
            rs = jnp.clip(r - kr_win // 2, 0, n_rows - kr_win)
            tiles = []
            for p in range(D_KEY_ROWS // 2):
                ka0 = ks + 2 * p
                e = jnp.clip(ka0 - r + NA_ROWS, 0, 2 * NA_ROWS - 1)
                ok0 = ((ka0 >= rs) & (ka0 < rs + kr_win)).astype(jnp.int32)
                ok1 = ((ka0 + 1 >= rs) & (ka0 + 1 < rs + kr_win)).astype(jnp.int32)
                ok = jnp.where(first, ok0, ok1) > 0
                tiles.append(jnp.where(ok, tt_ref[h, e], NEG))
            bias_rows.append(jnp.concatenate(tiles, axis=1))
        bias = jnp.concatenate(bias_rows, axis=0)
        msk = _slot_mask(qd.shape, HEAD_DIM, h)
        q = jnp.where(msk, qd, zero)
        s_l = _dot_t(q, kdw) + bias
        s_c = _dot_t(q, kd_c)
        o = _softmax_pv([s_l, s_c], [vdw, vd_c])
        od = jnp.where(msk, o, od)
    o_ref[:, 768:1024] = od.astype(BF16)


def _attn_lat_call(feat, sink, layer, caches, lw, consts, tt, *, n_prompt):
    first = n_prompt // DEC_SEQ
    nb = feat.shape[0] // DEC_SEQ - first
    full = lambda a: pl.BlockSpec(a.shape, lambda b, j: (0,) * a.ndim)
    lyr = lambda a: pl.BlockSpec((None,) + a.shape[1:], lambda b, j: (layer,) + (0,) * (a.ndim - 1))
    cache = lambda a: pl.BlockSpec((None, None) + a.shape[2:], lambda b, j: (b, layer, 0, 0))
    tabs = [consts["cl_lat"], consts["sl_lat"], consts["ccb_lat"], consts["scb_lat"]]
    in_specs = [pl.BlockSpec(memory_space=pltpu.SMEM),
                pl.BlockSpec((DEC_SEQ, FEAT), lambda b, j: (first + b, 0))]
    in_specs += [cache(a) for a in caches]
    in_specs += [lyr(lw["w_ukv"]), lyr(lw["kn_a"]), full(consts["ones128"]),
                 pl.BlockSpec((N_HEADS,) + tt.shape[1:], lambda b, j: (layer, 0, 0, 0))]
    in_specs += [full(a) for a in tabs]
    return pl.pallas_call(
        _attn_lat_kernel,
        grid=(nb, DEC_SEQ // Q_BLOCK),
        in_specs=in_specs,
        out_specs=pl.BlockSpec((Q_BLOCK, D_MODEL), lambda b, j: (b * (DEC_SEQ // Q_BLOCK) + j, 0)),
        out_shape=jax.ShapeDtypeStruct((nb * DEC_SEQ, D_MODEL), BF16),
        scratch_shapes=[pltpu.VMEM((PAST_LEN, 512), BF16), pltpu.VMEM((PAST_LEN, 256), BF16),
                        pltpu.VMEM((DEC_SEQ, 256), BF16), pltpu.VMEM((DEC_SEQ, 256), BF16)],
        compiler_params=_cparams(("parallel", "arbitrary")),
        name="attn_lat",
    )(sink, feat, *caches, lw["w_ukv"], lw["kn_a"], consts["ones128"], tt, *tabs)


def _layer_weights(w_in, w_uq, w_ukv, w_o, g_mix, g_qa, g_kva, qn_a, kn_a, qn_b, kn_b, qn_d, kn_d):
    z = lambda *s: jnp.zeros(s, F32)
    off = 416
    w_in_p = jnp.concatenate(
        [w_in[..., :384], w_in[..., 384:416], z(DEPTH, D_MODEL, LANES - A_ROPE)]
        + [w_in[..., off + HEAD_DIM * h:off + HEAD_DIM * (h + 1)] for h in B_ORDER]
        + [w_in[..., 672:]], axis=-1).astype(BF16)

    def slots(a, rows):
        parts = []
        for h in range(N_HEADS):
            hd = a[..., A_QK * h:A_QK * (h + 1)]
            parts += [hd[..., HEAD_DIM:], z(*rows, LANES - A_QK), hd[..., :HEAD_DIM]]
        return jnp.concatenate(parts, axis=-1)

    w_uq_p = slots(w_uq, (DEPTH, 256)).astype(BF16)
    k_cols, v_cols = [], []
    for h in range(N_HEADS):
        k_cols += [z(DEPTH, 128, HEAD_DIM), w_ukv[..., 128 * h:128 * h + HEAD_DIM]]
        v_cols.append(w_ukv[..., 128 * h + HEAD_DIM:128 * (h + 1)])
    w_ukv_p = jnp.concatenate(k_cols + v_cols, axis=-1).astype(BF16)
    w_o_p = jnp.concatenate(
        [w_o[:, :256]] + [w_o[:, 256 + HEAD_DIM * h:256 + HEAD_DIM * (h + 1)] for h in B_ORDER]
        + [w_o[:, 512:]], axis=1).astype(BF16)
    gain_a = lambda g: slots(jnp.tile(g, (1, N_HEADS)), (DEPTH,))[:, None, :]
    tile4 = lambda g: jnp.tile(g, (1, N_HEADS))[:, None, :]
    lw = dict(w_in=w_in_p, w_uq=w_uq_p, w_ukv=w_ukv_p,
              g_mix=g_mix[:, None, :], g_qa=g_qa[:, None, :], g_kva=g_kva[:, None, :],
              qn_a=gain_a(qn_a), kn_a=gain_a(kn_a), qn_b=tile4(qn_b), kn_b=tile4(kn_b),
              qn_d=tile4(qn_d), kn_d=tile4(kn_d))
    return lw, w_o_p


def _constants():
    consts = {}
    ident = lambda cs: (jnp.concatenate([jnp.ones((PREP_BLOCK, LANES), F32), cs[0]]),
                        jnp.concatenate([jnp.zeros((PREP_BLOCK, LANES), F32), cs[1]]))
    consts["cos_a"], consts["sin_a"] = ident(_rope_tables(A_ROPE, LANES, DEC_SEQ))
    consts["cos_b"], consts["sin_b"] = ident(_rope_tables(HEAD_DIM, HEAD_DIM, DEC_SEQ))
    for tag, n in (("ctx", SEQ), ("lat", DEC_SEQ)):
        cl, sl, ccb, scb = _fourier_constants(n)
        consts["cl_" + tag], consts["sl_" + tag], consts["ccb_" + tag], consts["scb_" + tag] = cl, sl, ccb, scb
    consts["ones128"] = _group_ones(512, LANES)
    consts["ones64"] = _group_ones(256, HEAD_DIM)
    return consts


def kernel(x_prompt, x_sample, cache_mla_ckv, cache_mla_krope, cache_win_k, cache_win_v, cache_na_k, cache_na_v, c, c_ctx, w_ada, b_ada, g_ffn1, w_gate1, w_up1, w_down1, g_mix, w_in, g_qa, w_uq, g_kva, w_ukv, qn_a, kn_a, qn_b, kn_b, sink_b, qn_d, kn_d, rpb_d, w_o, g_ffn2, w_gate2, w_up2, w_down2):
    batch, seq, _ = x_prompt.shape
    dec_batch, dec_seq, _ = x_sample.shape
    assert (seq, dec_seq, dec_batch + 1 <= MOD_ROWS) == (SEQ, DEC_SEQ, True)

    consts = _constants()
    lw, w_o_p = _layer_weights(w_in, w_uq, w_ukv, w_o, g_mix, g_qa, g_kva, qn_a, kn_a, qn_b, kn_b, qn_d, kn_d)

    cond = jnp.concatenate([c_ctx[None, :], c, jnp.zeros((MOD_ROWS - 1 - dec_batch, D_MODEL), F32)], axis=0)
    mod = _ada_call(cond, w_ada, b_ada)
    tt = _rpb_call(rpb_d)

    caches = [cache_mla_ckv,
              jnp.pad(cache_mla_krope, ((0, 0), (0, 0), (0, 0), (0, LANES - A_ROPE))),
              cache_win_k.reshape(dec_batch, DEPTH, PAST_LEN, 128),
              cache_win_v.reshape(dec_batch, DEPTH, PAST_LEN, 128),
              cache_na_k.reshape(dec_batch, DEPTH, PAST_LEN, 256),
              cache_na_v.reshape(dec_batch, DEPTH, PAST_LEN, 256)]

    n_prompt = batch * seq
    x = jnp.concatenate([x_prompt.reshape(n_prompt, D_MODEL), x_sample.reshape(dec_batch * dec_seq, D_MODEL)])
    new = [jnp.zeros((batch, DEPTH, seq, w), F32) for w in _CACHE_WIDTHS]
    for l in range(DEPTH):
        sink = sink_b[l]
        x = _ffn_call(x, mod, l, 0, g_ffn1, w_gate1, w_up1, w_down1, n_prompt=n_prompt)
        feat, new = _prep_call(x, mod, l, lw, consts, new, n_prompt=n_prompt)
        mix = (_attn_ctx_call(feat, sink, consts, n_prompt=n_prompt),
               _attn_lat_call(feat, sink, l, caches, lw, consts, tt, n_prompt=n_prompt))
        x = _ffn_call(x, mod, l, 2, g_ffn2, w_gate2, w_up2, w_down2, n_prompt=n_prompt, mix=mix, w_o=w_o_p)

    heads = lambda a, n: a.reshape(batch, DEPTH, seq, n, HEAD_DIM)
    return (x[:n_prompt].reshape(batch, seq, D_MODEL), x[n_prompt:].reshape(dec_batch, dec_seq, D_MODEL),
            new[0], new[1], heads(new[2], 2), heads(new[3], 2), heads(new[4], N_HEADS), heads(new[5], N_HEADS))
```

```python
import functools

import numpy as np
import jax
import jax.numpy as jnp
from jax import lax
from jax.experimental import pallas as pl
from jax.experimental.pallas import tpu as pltpu

F32 = jnp.float32
BF16 = jnp.bfloat16

D_MODEL = 1024
DEPTH = 4
SEQ = 256
DEC_SEQ = 1024
PAST_LEN = 256
GRID_W = 64
D_FF = 2816
N_MOD = 9
EPS = 1e-6
ROPE_BASE = 10000.0
A_QK = 96
A_ROPE = 32
B_WINDOW = 128
NA_ROWS = 8
NA_COLS = 16
HEAD_DIM = 64
N_HEADS = 4
NEG = -1e30

LANES = 128
MOD_ROWS = 8
VMEM_LIMIT = 56 * 1024 * 1024

QA, KA, VA = 0, 512, 1024
QB, KB, VB = 1280, 1536, 1664
XC = 1792
QD, KD, VD = 2048, 2304, 2560
FEAT = 2816
B_ORDER = (0, 2, 1, 3)

W_IN_KR, W_IN_QB, W_IN_KB, W_IN_WIDTH = 384, 416, 672, 1952
P_CQ, P_CKV, P_KR, P_QB, P_KB, P_VB, P_XC, P_QD, P_KD, P_VD = (
    0, 256, 384, 512, 768, 896, 1024, 1280, 1536, 1792)
P_WIDTH = 2048

TOK_BLOCK = 512
PREP_BLOCK = 256
FF_CHUNK = 256
N_FF_CHUNKS = D_FF // FF_CHUNK
Q_BLOCK = 256
B_KEYS = 512
D_KEY_ROWS = 12


def _cparams(sem):
    return pltpu.CompilerParams(dimension_semantics=sem, vmem_limit_bytes=VMEM_LIMIT)


def _dot(a, b):
    return jnp.dot(a, b, preferred_element_type=F32)


def _dot_t(a, b):
    return lax.dot_general(a, b, (((1,), (1,)), ((), ())), preferred_element_type=F32)


def _group_sums(sq, ones_bf16):
    hi = sq.astype(BF16)
    lo = (sq - hi.astype(F32)).astype(BF16)
    return _dot(hi, ones_bf16) + _dot(lo, ones_bf16)


def _rope_tables(width, slot, n_pos):
    half = width // 2
    quarter = half // 2
    inv = (np.float32(ROPE_BASE) ** (-np.arange(0, half, 2, dtype=np.float32) / np.float32(half))).astype(np.float32)
    t = np.arange(n_pos)
    pos = np.stack([t // GRID_W, t % GRID_W], axis=0).astype(np.float32)
    cos = np.ones((n_pos, LANES), np.float32)
    sin = np.zeros((n_pos, LANES), np.float32)
    for lane in range(LANES):
        i = lane % slot
        if i >= width:
            continue
        axis = i // half
        j = i % half
        ang = pos[axis] * inv[j % quarter]
        cos[:, lane] = np.cos(ang)
        sin[:, lane] = np.sin(ang) * (-1.0 if j < quarter else 1.0)
    return jnp.asarray(cos), jnp.asarray(sin)


def _dft_tables(n):
    k = np.arange(n)
    ang = 2.0 * np.pi * ((k[:, None] * k[None, :]) % n) / n
    return np.cos(ang), np.sin(ang)


def _fourier_constants(n_pos):
    cl, sl = _dft_tables(n_pos)
    cc, sc = _dft_tables(HEAD_DIM)
    scale = 1.0 / np.sqrt(float(HEAD_DIM * n_pos))
    eye = np.eye(N_HEADS)
    ccb = np.kron(eye, cc) * scale
    scb = np.kron(eye, sc) * scale
    as_bf = lambda a: jnp.asarray(a.astype(np.float32)).astype(BF16)
    return as_bf(cl), as_bf(sl), as_bf(ccb), as_bf(scb)


def _group_ones(width, group):
    g = np.arange(width) // group
    return jnp.asarray((g[:, None] == g[None, :]).astype(np.float32)).astype(BF16)


def _ada_kernel(cond_ref, w_ref, b_ref, o_ref):
    c = cond_ref[...]
    s = (c * jax.nn.sigmoid(c)).astype(BF16)
    o_ref[...] = _dot(s, w_ref[...].astype(BF16)) + b_ref[...]


def _ada_call(cond, w_ada, b_ada):
    tn = 1024
    n = N_MOD * D_MODEL
    return pl.pallas_call(
        _ada_kernel,
        grid=(DEPTH, n // tn),
        in_specs=[
            pl.BlockSpec((MOD_ROWS, D_MODEL), lambda l, j: (0, 0)),
            pl.BlockSpec((None, D_MODEL, tn), lambda l, j: (l, 0, j)),
            pl.BlockSpec((None, 1, tn), lambda l, j: (l, 0, j)),
        ],
        out_specs=pl.BlockSpec((None, MOD_ROWS, tn), lambda l, j: (l, 0, j)),
        out_shape=jax.ShapeDtypeStruct((DEPTH, MOD_ROWS, n), F32),
        compiler_params=_cparams(("parallel", "parallel")),
        name="ada_mod",
    )(cond, w_ada, b_ada.reshape(DEPTH, 1, n))


def _rpb_kernel(rpb_ref, o_ref):
    g = pl.program_id(0)
    n_dr = 2 * NA_ROWS - 1
    n_dc = 2 * NA_COLS - 1
    base = g * (n_dr * n_dc)
    shape = (GRID_W, LANES)
    q = lax.broadcasted_iota(jnp.int32, shape, 0)
    lane = lax.broadcasted_iota(jnp.int32, shape, 1)
    w = lane & (GRID_W - 1)
    dc = jnp.clip(w - q + (NA_COLS - 1), 0, n_dc - 1)
    cs = jnp.clip(q - NA_COLS // 2, 0, GRID_W - NA_COLS)
    col_ok = (w >= cs) & (w < cs + NA_COLS)
    neg = jnp.full(shape, NEG, F32)
    tiles = []
    for d in range(n_dr):
        acc = jnp.zeros(shape, F32)
        for j in range(n_dc):
            acc = jnp.where(dc == j, rpb_ref[base + d * n_dc + j], acc)
        tiles.append(jnp.where(col_ok, acc, neg))
    first = lane < GRID_W
    for e in range(n_dr + 1):
        left = tiles[e - 1] if e >= 1 else neg
        right = tiles[e] if e < n_dr else neg
        o_ref[e] = jnp.where(first, left, right)


def _rpb_call(rpb_d):
    n_e = 2 * NA_ROWS
    return pl.pallas_call(
        _rpb_kernel,
        grid=(DEPTH * N_HEADS,),
        in_specs=[pl.BlockSpec(memory_space=pltpu.SMEM)],
        out_specs=pl.BlockSpec((None, n_e, GRID_W, LANES), lambda g: (g, 0, 0, 0)),
        out_shape=jax.ShapeDtypeStruct((DEPTH * N_HEADS, n_e, GRID_W, LANES), F32),
        compiler_params=_cparams(("parallel",)),
        name="rpb_tiles",
    )(rpb_d.reshape(-1))


def _mod_row(b, n_prompt_blocks, blocks_per_batch):
    return jnp.where(b < n_prompt_blocks, 0, 1 + (b - n_prompt_blocks) // blocks_per_batch)


def _ffn_kernel(*refs, layer, n_prompt_blocks, blocks_per_batch, with_mix, split_in, split_out):
    n_x = 2 if split_in else 1
    x_refs, refs = refs[:n_x], refs[n_x:]
    if with_mix:
        mixp_ref, mixs_ref, wo_ref, g2_ref = refs[:4]
        refs = refs[4:]
    sh_ref, sc_ref, g_ref, gn_ref, wg_hbm, wu_hbm, wd_hbm = refs[:7]
    n_o = 2 if split_out else 1
    o_refs = refs[7:7 + n_o]
    wg_scr, wu_scr, wd_scr, stg_g, stg_u, stg_d, sem, x_scr, a_scr = refs[7 + n_o:]
    b = pl.program_id(0)
    is_prompt = b < n_prompt_blocks
    row = _mod_row(b, n_prompt_blocks, blocks_per_batch)

    def weight_copies(c, slot):
        cols = pl.ds(c * FF_CHUNK, FF_CHUNK)
        return (pltpu.make_async_copy(wg_hbm.at[layer, :, cols], stg_g.at[slot], sem.at[0, slot]),
                pltpu.make_async_copy(wu_hbm.at[layer, :, cols], stg_u.at[slot], sem.at[1, slot]),
                pltpu.make_async_copy(wd_hbm.at[layer, cols, :], stg_d.at[slot], sem.at[2, slot]))

    def block(first):
        if first:
            for slot in range(2):
                for cp in weight_copies(slot, slot):
                    cp.start()
        x = jnp.where(is_prompt, x_refs[0][...], x_refs[1][...]) if split_in else x_refs[0][...]
        if with_mix:
            mix = jnp.where(is_prompt, mixp_ref[...], mixs_ref[...])
            x = x + g2_ref[pl.ds(row, 1), :] * _dot(mix, wo_ref[...])
        x_scr[...] = x
        y = x * lax.rsqrt(jnp.mean(x * x, axis=-1, keepdims=True) + EPS) * gn_ref[...]
        h = (y * (1.0 + sc_ref[pl.ds(row, 1), :]) + sh_ref[pl.ds(row, 1), :]).astype(BF16)
        for c in range(N_FF_CHUNKS):
            lanes = slice(c * FF_CHUNK, (c + 1) * FF_CHUNK)
            if first:
                slot = c % 2
                for cp in weight_copies(c, slot):
                    cp.wait()
                wg_scr[:, lanes] = stg_g[slot].astype(BF16)
                wu_scr[:, lanes] = stg_u[slot].astype(BF16)
                wd_scr[lanes, :] = stg_d[slot].astype(BF16)
                if c + 2 < N_FF_CHUNKS:
                    for cp in weight_copies(c + 2, slot):
                        cp.start()
            gate = _dot(h, wg_scr[:, lanes])
            up = _dot(h, wu_scr[:, lanes])
            a_scr[:, lanes] = ((gate * jax.nn.sigmoid(gate)) * up).astype(BF16)
        out = x_scr[...] + (0.5 * g_ref[pl.ds(row, 1), :]) * _dot(a_scr[...], wd_scr[...])
        if not split_out:
            o_refs[0][...] = out
        elif first:
            o_refs[0][...] = out
        else:
            @pl.when(is_prompt)
            def _():
                o_refs[0][...] = out

            @pl.when(jnp.logical_not(is_prompt))
            def _():
                o_refs[1][...] = out

    pl.when(b == 0)(functools.partial(block, True))
    pl.when(b != 0)(functools.partial(block, False))


def _ffn_call(x, mod, layer, slot, g_norm, w_gate, w_up, w_down, *, n_prompt, mix=None, w_o=None, split_out=False):
    split_in = isinstance(x, tuple)
    t = x[0].shape[0] + x[1].shape[0] if split_in else x.shape[0]
    tm = TOK_BLOCK
    npb = n_prompt // tm
    with_mix = mix is not None
    mod_spec = lambda k: pl.BlockSpec((None, MOD_ROWS, D_MODEL), lambda b: (layer, 0, k))
    x_spec = pl.BlockSpec((tm, D_MODEL), lambda b: (b, 0))
    p_spec = pl.BlockSpec((tm, D_MODEL), lambda b: (jnp.minimum(b, npb - 1), 0))
    s_spec = pl.BlockSpec((tm, D_MODEL), lambda b: (jnp.maximum(b - npb, 0), 0))
    hbm = pl.BlockSpec(memory_space=pl.ANY)
    in_specs = [p_spec, s_spec] if split_in else [x_spec]
    args = list(x) if split_in else [x]
    if with_mix:
        in_specs += [p_spec, s_spec, pl.BlockSpec((None, D_MODEL, D_MODEL), lambda b: (layer, 0, 0)), mod_spec(5)]
        args += [mix[0], mix[1], w_o, mod]
    c0 = 3 * slot
    in_specs += [mod_spec(c0), mod_spec(c0 + 1), mod_spec(c0 + 2),
                 pl.BlockSpec((None, 1, D_MODEL), lambda b: (layer, 0, 0)), hbm, hbm, hbm]
    args += [mod, mod, mod, g_norm.reshape(DEPTH, 1, D_MODEL), w_gate, w_up, w_down]
    scratch = [pltpu.VMEM((D_MODEL, D_FF), BF16), pltpu.VMEM((D_MODEL, D_FF), BF16), pltpu.VMEM((D_FF, D_MODEL), BF16),
               pltpu.VMEM((2, D_MODEL, FF_CHUNK), F32), pltpu.VMEM((2, D_MODEL, FF_CHUNK), F32),
               pltpu.VMEM((2, FF_CHUNK, D_MODEL), F32), pltpu.SemaphoreType.DMA((3, 2)),
               pltpu.VMEM((tm, D_MODEL), F32), pltpu.VMEM((tm, D_FF), BF16)]
    if split_out:
        out_specs = [p_spec, s_spec]
        out_shape = [jax.ShapeDtypeStruct((n_prompt, D_MODEL), F32), jax.ShapeDtypeStruct((t - n_prompt, D_MODEL), F32)]
    else:
        out_specs = x_spec
        out_shape = jax.ShapeDtypeStruct((t, D_MODEL), F32)
    return pl.pallas_call(
        functools.partial(_ffn_kernel, layer=layer, n_prompt_blocks=npb, blocks_per_batch=DEC_SEQ // tm,
                          with_mix=with_mix, split_in=split_in, split_out=split_out),
        grid=(t // tm,),
        in_specs=in_specs,
        out_specs=out_specs,
        out_shape=out_shape,
        scratch_shapes=scratch,
        compiler_params=_cparams(("arbitrary",)),
        name="ffn_mix" if with_mix else "ffn",
    )(*args)


def _swap(x, width):
    lane = lax.broadcasted_iota(jnp.int32, x.shape, 1)
    first = (lane & width) == 0
    return jnp.where(first, pltpu.roll(x, LANES - width, 1), pltpu.roll(x, width, 1))


def _rope(x, cos, sin, width):
    parts = []
    for c in range(x.shape[1] // LANES):
        xc = x[:, c * LANES:(c + 1) * LANES]
        parts.append(xc * cos + _swap(xc, width) * sin)
    return parts[0] if len(parts) == 1 else jnp.concatenate(parts, axis=1)


def _head_norm(x, ones, gain, inv_dim):
    ms = _group_sums(x * x, ones) * inv_dim
    return x * lax.rsqrt(ms + EPS) * gain


def _mla_keys(ckv_n_bf16, kr_pad, w_ukv, ones128, kn_a):
    kv = _dot(ckv_n_bf16, w_ukv)
    k = kv[:, :512] + jnp.concatenate([kr_pad] * N_HEADS, axis=1)
    return _head_norm(k, ones128, kn_a, 1.0 / A_QK), kv[:, 512:]


def _prep_kernel(*refs, n_prompt_blocks, blocks_per_batch, n_alias):
    (x_ref, sh_ref, sc_ref, gn_ref, win_ref, gqa_ref, wuq_ref, gkva_ref, wukv_ref,
     qna_ref, kna_ref, qnb_ref, knb_ref, qnd_ref, knd_ref,
     ones128_ref, ones64_ref, cos_a_ref, sin_a_ref, cos_b_ref, sin_b_ref) = refs[:21]
    feat_ref, ckv_ref, kr_ref, wk_ref, wv_ref, nk_ref, nv_ref, win_scr = refs[21 + n_alias:]
    b = pl.program_id(0)
    row = _mod_row(b, n_prompt_blocks, blocks_per_batch)
    is_prompt = b < n_prompt_blocks

    @pl.when(b == 0)
    def _():
        src_q = W_IN_QB
        for r in range(0, D_MODEL, PREP_BLOCK):
            rows = slice(r, r + PREP_BLOCK)
            win_scr[rows, 0:P_KR + A_ROPE] = win_ref[rows, 0:W_IN_KR + A_ROPE].astype(BF16)
            win_scr[rows, P_KR + A_ROPE:P_QB] = jnp.zeros((PREP_BLOCK, LANES - A_ROPE), BF16)
            for p, hd in enumerate(B_ORDER):
                win_scr[rows, P_QB + HEAD_DIM * p:P_QB + HEAD_DIM * (p + 1)] = (
                    win_ref[rows, src_q + HEAD_DIM * hd:src_q + HEAD_DIM * (hd + 1)].astype(BF16))
            win_scr[rows, P_KB:P_WIDTH] = win_ref[rows, W_IN_KB:W_IN_WIDTH].astype(BF16)

    x = x_ref[...]
    y = x * lax.rsqrt(jnp.mean(x * x, axis=-1, keepdims=True) + EPS) * gn_ref[...]
    h = (y * (1.0 + sc_ref[pl.ds(row, 1), :]) + sh_ref[pl.ds(row, 1), :]).astype(BF16)
    proj = _dot(h, win_scr[...])
    ones128 = ones128_ref[...]
    ones64 = ones64_ref[...]

    cq = proj[:, P_CQ:P_CQ + 256]
    cqn = cq * lax.rsqrt(jnp.mean(cq * cq, axis=-1, keepdims=True) + EPS) * gqa_ref[...]
    qa = _head_norm(_dot(cqn.astype(BF16), wuq_ref[...]), ones128, qna_ref[...], 1.0 / A_QK)
    ckv = proj[:, P_CKV:P_CKV + 128]
    ckv_n = ckv * lax.rsqrt(jnp.mean(ckv * ckv, axis=-1, keepdims=True) + EPS) * gkva_ref[...]
    kr_pad = proj[:, P_KR:P_KR + 128]
    ka, va = _mla_keys(ckv_n.astype(BF16), kr_pad, wukv_ref[...], ones128, kna_ref[...])
    cos_a, sin_a = cos_a_ref[...], sin_a_ref[...]
    qa = _rope(qa, cos_a, sin_a, 8)
    ka = _rope(ka, cos_a, sin_a, 8)
    feat_ref[:, QA:QA + 512] = (qa * (A_QK ** -0.5)).astype(BF16)
    feat_ref[:, KA:KA + 512] = ka.astype(BF16)
    feat_ref[:, VA:VA + 256] = va.astype(BF16)

    qb = _head_norm(proj[:, P_QB:P_QB + 256], ones64, qnb_ref[...], 1.0 / HEAD_DIM)
    kb = _head_norm(proj[:, P_KB:P_KB + 128], ones64[:128, :128], knb_ref[:, :128], 1.0 / HEAD_DIM)
    vb = proj[:, P_VB:P_VB + 128]
    cos_b, sin_b = cos_b_ref[...], sin_b_ref[...]
    qb = _rope(qb, cos_b, sin_b, 16)
    kb = _rope(kb, cos_b, sin_b, 16)
    feat_ref[:, QB:QB + 256] = (qb * (HEAD_DIM ** -0.5)).astype(BF16)
    feat_ref[:, KB:KB + 128] = kb.astype(BF16)
    feat_ref[:, VB:VB + 128] = vb.astype(BF16)

    feat_ref[:, XC:XC + 256] = proj[:, P_XC:P_XC + 256].astype(BF16)

    qd = _head_norm(proj[:, P_QD:P_QD + 256], ones64, qnd_ref[...], 1.0 / HEAD_DIM)
    kd = _head_norm(proj[:, P_KD:P_KD + 256], ones64, knd_ref[...], 1.0 / HEAD_DIM)
    vd = proj[:, P_VD:P_VD + 256]
    feat_ref[:, QD:QD + 256] = (qd * (HEAD_DIM ** -0.5)).astype(BF16)
    feat_ref[:, KD:KD + 256] = kd.astype(BF16)
    feat_ref[:, VD:VD + 256] = vd.astype(BF16)

    @pl.when(is_prompt)
    def _():
        ckv_ref[...] = ckv_n
        kr_ref[...] = kr_pad[:, :A_ROPE]
        wk_ref[...] = kb
        wv_ref[...] = vb
        nk_ref[...] = kd
        nv_ref[...] = vd


_PREP_WEIGHTS = ("g_mix", "w_in", "g_qa", "w_uq", "g_kva", "w_ukv",
                 "qn_a", "kn_a", "qn_b", "kn_b", "qn_d", "kn_d")
_CACHE_WIDTHS = (128, A_ROPE, 128, 128, 256, 256)


def _prep_call(x, mod, layer, lw, consts, new_caches, *, n_prompt):
    t = x.shape[0]
    tm = PREP_BLOCK
    npb = n_prompt // tm
    bpb = DEC_SEQ // tm
    batch = n_prompt // SEQ
    full = lambda a: pl.BlockSpec(a.shape, lambda b: (0,) * a.ndim)
    mod_spec = lambda k: pl.BlockSpec((None, MOD_ROWS, D_MODEL), lambda b: (layer, 0, k))
    lyr = lambda a: pl.BlockSpec((None,) + a.shape[1:], lambda b: (layer,) + (0,) * (a.ndim - 1))
    tok = lambda w: pl.BlockSpec((tm, w), lambda b: (b, 0))
    pos = pl.BlockSpec((tm, LANES), lambda b: (jnp.where(b < npb, 0, 1 + (b - npb) % bpb), 0))
    weights = [lw[k] for k in _PREP_WEIGHTS]
    tabs = [consts["cos_a"], consts["sin_a"], consts["cos_b"], consts["sin_b"]]
    in_specs = [tok(D_MODEL), mod_spec(3), mod_spec(4)] + [lyr(a) for a in weights]
    in_specs += [full(consts["ones128"]), full(consts["ones64"])] + [pos] * len(tabs)
    args = [x, mod, mod] + weights + [consts["ones128"], consts["ones64"]] + tabs
    aliases = {len(args) + k: 1 + k for k in range(len(new_caches))}
    in_specs += [pl.BlockSpec(memory_space=pl.ANY)] * len(new_caches)
    args += list(new_caches)
    spb = SEQ // tm
    cache_spec = lambda w: pl.BlockSpec(
        (None, None, tm, w), lambda b: (jnp.minimum(b, npb - 1) // spb, layer, jnp.minimum(b, npb - 1) % spb, 0))
    out_specs = [tok(FEAT)] + [cache_spec(w) for w in _CACHE_WIDTHS]
    out_shape = [jax.ShapeDtypeStruct((t, FEAT), BF16)]
    out_shape += [jax.ShapeDtypeStruct((batch, DEPTH, SEQ, w), F32) for w in _CACHE_WIDTHS]
    feat, *caches = pl.pallas_call(
        functools.partial(_prep_kernel, n_prompt_blocks=npb, blocks_per_batch=bpb, n_alias=len(aliases)),
        grid=(t // tm,),
        in_specs=in_specs,
        out_specs=out_specs,
        out_shape=out_shape,
        input_output_aliases=aliases,
        scratch_shapes=[pltpu.VMEM((D_MODEL, P_WIDTH), BF16)],
        compiler_params=_cparams(("arbitrary",)),
        name="prep",
    )(*args)
    return feat, caches


def _slot_mask(shape, width, slot):
    lane = lax.broadcasted_iota(jnp.int32, shape, 1)
    return (lane >= slot * width) & (lane < (slot + 1) * width)


def _softmax_pv(scores, values, sink=None):
    m = functools.reduce(jnp.maximum, [jnp.max(s, axis=-1, keepdims=True) for s in scores])
    if sink is not None:
        m = jnp.maximum(m, sink)
    ps = [jnp.exp(s - m) for s in scores]
    den = functools.reduce(jnp.add, [jnp.sum(p, axis=-1, keepdims=True) for p in ps])
    if sink is not None:
        den = den + jnp.exp(sink - m)
    o = functools.reduce(jnp.add, [_dot(p.astype(BF16), v) for p, v in zip(ps, values)])
    return o / den


def _attn_ctx_kernel(sink_ref, feat_ref, cl_ref, sl_ref, ccb_ref, scb_ref, o_ref):
    n = feat_ref.shape[0]
    zero = jnp.zeros((), BF16)

    va = feat_ref[:, VA:VA + 256]
    oa = jnp.zeros((n, 256), F32)
    for h in range(N_HEADS):
        q = feat_ref[:, QA + 128 * h:QA + 128 * (h + 1)]
        k = feat_ref[:, KA + 128 * h:KA + 128 * (h + 1)]
        o = _softmax_pv([_dot_t(q, k)], [va])
        oa = jnp.where(_slot_mask(o.shape, HEAD_DIM, h), o, oa)
    o_ref[:, 0:256] = oa.astype(BF16)

    kb = feat_ref[:, KB:KB + 128]
    vb = feat_ref[:, VB:VB + 128]
    for c in range(2):
        qc = feat_ref[:, QB + 128 * c:QB + 128 * (c + 1)]
        ob = jnp.zeros((n, 128), F32)
        for half in range(2):
            head = B_ORDER[2 * c + half]
            msk = _slot_mask(qc.shape, HEAD_DIM, half)
            o = _softmax_pv([_dot_t(jnp.where(msk, qc, zero), kb)], [vb], sink=sink_ref[head])
            ob = jnp.where(msk, o, ob)
        o_ref[:, 256 + 128 * c:256 + 128 * (c + 1)] = ob.astype(BF16)

    xc = feat_ref[:, XC:XC + 256]
    fa = _dot(xc, ccb_ref[...]).astype(BF16)
    fb = _dot(xc, scb_ref[...]).astype(BF16)
    o_ref[:, 512:768] = (_dot(cl_ref[...], fa) - _dot(sl_ref[...], fb)).astype(BF16)

    qd = feat_ref[:, QD:QD + 256]
    kd = feat_ref[:, KD:KD + 256]
    vd = feat_ref[:, VD:VD + 256]
    od = jnp.zeros((n, 256), F32)
    for h in range(N_HEADS):
        msk = _slot_mask(qd.shape, HEAD_DIM, h)
        o = _softmax_pv([_dot_t(jnp.where(msk, qd, zero), kd)], [vd])
        od = jnp.where(msk, o, od)
    o_ref[:, 768:1024] = od.astype(BF16)


def _attn_ctx_call(feat, sink, consts, *, n_prompt):
    t = n_prompt
    full = lambda a: pl.BlockSpec(a.shape, lambda i: (0,) * a.ndim)
    tabs = [consts["cl_ctx"], consts["sl_ctx"], consts["ccb_ctx"], consts["scb_ctx"]]
    return pl.pallas_call(
        _attn_ctx_kernel,
        grid=(t // SEQ,),
        in_specs=[pl.BlockSpec(memory_space=pltpu.SMEM),
                  pl.BlockSpec((SEQ, FEAT), lambda i: (i, 0))] + [full(a) for a in tabs],
        out_specs=pl.BlockSpec((SEQ, D_MODEL), lambda i: (i, 0)),
        out_shape=jax.ShapeDtypeStruct((t, D_MODEL), BF16),
        compiler_params=_cparams(("parallel",)),
        name="attn_ctx",
    )(sink, feat, *tabs)


def _attn_lat_kernel(sink_ref, feat_ref, cckv_ref, ckr_ref, cwk_ref, cwv_ref, cnk_ref, cnv_ref,
                     wukv_ref, kna_ref, ones128_ref, tt_ref, cl_ref, sl_ref, ccb_ref, scb_ref,
                     o_ref, kac_scr, vac_scr, fa_scr, fb_scr):
    j = pl.program_id(1)
    nq = Q_BLOCK
    zero = jnp.zeros((), BF16)

    @pl.when(j == 0)
    def _():
        ka_c, va_c = _mla_keys(cckv_ref[...].astype(BF16), ckr_ref[...], wukv_ref[...],
                               ones128_ref[...], kna_ref[...])
        kac_scr[...] = ka_c.astype(BF16)
        vac_scr[...] = va_c.astype(BF16)
        xc = feat_ref[:, XC:XC + 256]
        fa_scr[...] = _dot(xc, ccb_ref[...]).astype(BF16)
        fb_scr[...] = _dot(xc, scb_ref[...]).astype(BF16)

    r0 = pl.multiple_of(j * nq, nq)
    rows = pl.ds(r0, nq)

    va = feat_ref[:, VA:VA + 256]
    va_c = vac_scr[...]
    oa = jnp.zeros((nq, 256), F32)
    for h in range(N_HEADS):
        q = feat_ref[rows, QA + 128 * h:QA + 128 * (h + 1)]
        s_c = _dot_t(q, kac_scr[:, 128 * h:128 * (h + 1)])
        s_l = _dot_t(q, feat_ref[:, KA + 128 * h:KA + 128 * (h + 1)])
        o = _softmax_pv([s_c, s_l], [va_c, va])
        oa = jnp.where(_slot_mask(o.shape, HEAD_DIM, h), o, oa)
    o_ref[:, 0:256] = oa.astype(BF16)

    start = pl.multiple_of(jnp.clip(r0 - B_WINDOW, 0, DEC_SEQ - B_KEYS), B_WINDOW)
    kw = feat_ref[pl.ds(start, B_KEYS), KB:KB + 128]
    vw = feat_ref[pl.ds(start, B_KEYS), VB:VB + 128]
    kb_c = cwk_ref[...].astype(BF16)
    vb_c = cwv_ref[...].astype(BF16)
    qpos = r0 + lax.broadcasted_iota(jnp.int32, (nq, B_KEYS), 0)
    kpos = start + lax.broadcasted_iota(jnp.int32, (nq, B_KEYS), 1)
    band = jnp.abs(qpos - kpos) <= B_WINDOW
    for c in range(2):
        qc = feat_ref[rows, QB + 128 * c:QB + 128 * (c + 1)]
        ob = jnp.zeros((nq, 128), F32)
        for half in range(2):
            head = B_ORDER[2 * c + half]
            msk = _slot_mask(qc.shape, HEAD_DIM, half)
            q = jnp.where(msk, qc, zero)
            s_l = jnp.where(band, _dot_t(q, kw), NEG)
            s_c = _dot_t(q, kb_c)
            o = _softmax_pv([s_l, s_c], [vw, vb_c], sink=sink_ref[head])
            ob = jnp.where(msk, o, ob)
        o_ref[:, 256 + 128 * c:256 + 128 * (c + 1)] = ob.astype(BF16)

    o_ref[:, 512:768] = (_dot(cl_ref[rows, :], fa_scr[...]) - _dot(sl_ref[rows, :], fb_scr[...])).astype(BF16)

    rows_per_block = nq // GRID_W
    n_rows = DEC_SEQ // GRID_W
    kr_win = min(NA_ROWS, n_rows)
    ks = jnp.where(j < 2, 0, n_rows - D_KEY_ROWS)
    k0_tok = pl.multiple_of(ks * GRID_W, GRID_W)
    keys = pl.ds(k0_tok, D_KEY_ROWS * GRID_W)
    kdw = feat_ref[keys, KD:KD + 256]
    vdw = feat_ref[keys, VD:VD + 256]
    kd_c = cnk_ref[...].astype(BF16)
    vd_c = cnv_ref[...].astype(BF16)
    qd = feat_ref[rows, QD:QD + 256]
    first = lax.broadcasted_iota(jnp.int32, (GRID_W, LANES), 1) < GRID_W
    od = jnp.zeros((nq, 256), F32)
    for h in range(N_HEADS):
        bias_rows = []
        for rr in range(rows_per_block):
            r = j * rows_per_block + rr
            rs = jnp.clip(r - kr_win // 2, 0, n_rows - kr_win)
            tiles = []
            for p in range(D_KEY_ROWS // 2):
                ka0 = ks + 2 * p
                e = jnp.clip(ka0 - r + NA_ROWS, 0, 2 * NA_ROWS - 1)
                ok0 = ((ka0 >= rs) & (ka0 < rs + kr_win)).astype(jnp.int32)
                ok1 = ((ka0 + 1 >= rs) & (ka0 + 1 < rs + kr_win)).astype(jnp.int32)
                ok = jnp.where(first, ok0, ok1) > 0
                tiles.append(jnp.where(ok, tt_ref[h, e], NEG))
            bias_rows.append(jnp.concatenate(tiles, axis=1))
        bias = jnp.concatenate(bias_rows, axis=0)
        msk = _slot_mask(qd.shape, HEAD_DIM, h)
        q = jnp.where(msk, qd, zero)
        s_l = _dot_t(q, kdw) + bias
        s_c = _dot_t(q, kd_c)
        o = _softmax_pv([s_l, s_c], [vdw, vd_c])
        od = jnp.where(msk, o, od)
    o_ref[:, 768:1024] = od.astype(BF16)


def _attn_lat_call(feat, sink, layer, caches, lw, consts, tt, *, n_prompt):
    first = n_prompt // DEC_SEQ
    nb = feat.shape[0] // DEC_SEQ - first
    full = lambda a: pl.BlockSpec(a.shape, lambda b, j: (0,) * a.ndim)
    lyr = lambda a: pl.BlockSpec((None,) + a.shape[1:], lambda b, j: (layer,) + (0,) * (a.ndim - 1))
    cache = lambda a: pl.BlockSpec((None, None) + a.shape[2:], lambda b, j: (b, layer, 0, 0))
    tabs = [consts["cl_lat"], consts["sl_lat"], consts["ccb_lat"], consts["scb_lat"]]
    in_specs = [pl.BlockSpec(memory_space=pltpu.SMEM),
                pl.BlockSpec((DEC_SEQ, FEAT), lambda b, j: (first + b, 0))]
    in_specs += [cache(a) for a in caches]
    in_specs += [lyr(lw["w_ukv"]), lyr(lw["kn_a"]), full(consts["ones128"]),
                 pl.BlockSpec((N_HEADS,) + tt.shape[1:], lambda b, j: (layer, 0, 0, 0))]
    in_specs += [full(a) for a in tabs]
    return pl.pallas_call(
        _attn_lat_kernel,
        grid=(nb, DEC_SEQ // Q_BLOCK),
        in_specs=in_specs,
        out_specs=pl.BlockSpec((Q_BLOCK, D_MODEL), lambda b, j: (b * (DEC_SEQ // Q_BLOCK) + j, 0)),
        out_shape=jax.ShapeDtypeStruct((nb * DEC_SEQ, D_MODEL), BF16),
        scratch_shapes=[pltpu.VMEM((PAST_LEN, 512), BF16), pltpu.VMEM((PAST_LEN, 256), BF16),
                        pltpu.VMEM((DEC_SEQ, 256), BF16), pltpu.VMEM((DEC_SEQ, 256), BF16)],
        compiler_params=_cparams(("parallel", "arbitrary")),
        name="attn_lat",
    )(sink, feat, *caches, lw["w_ukv"], lw["kn_a"], consts["ones128"], tt, *tabs)


def _layer_weights(w_in, w_uq, w_ukv, w_o, g_mix, g_qa, g_kva, qn_a, kn_a, qn_b, kn_b, qn_d, kn_d):
    z = lambda *s: jnp.zeros(s, F32)

    def slots(a, rows):
        parts = []
        for h in range(N_HEADS):
            hd = a[..., A_QK * h:A_QK * (h + 1)]
            parts += [hd[..., HEAD_DIM:], z(*rows, LANES - A_QK), hd[..., :HEAD_DIM]]
        return jnp.concatenate(parts, axis=-1)

    w_uq_p = slots(w_uq, (DEPTH, 256)).astype(BF16)
    k_cols, v_cols = [], []
    for h in range(N_HEADS):
        k_cols += [z(DEPTH, 128, HEAD_DIM), w_ukv[..., 128 * h:128 * h + HEAD_DIM]]
        v_cols.append(w_ukv[..., 128 * h + HEAD_DIM:128 * (h + 1)])
    w_ukv_p = jnp.concatenate(k_cols + v_cols, axis=-1).astype(BF16)
    w_o_p = jnp.concatenate(
        [w_o[:, :256]] + [w_o[:, 256 + HEAD_DIM * h:256 + HEAD_DIM * (h + 1)] for h in B_ORDER]
        + [w_o[:, 512:]], axis=1).astype(BF16)
    gain_a = lambda g: slots(jnp.tile(g, (1, N_HEADS)), (DEPTH,))[:, None, :]
    tile4 = lambda g: jnp.tile(g, (1, N_HEADS))[:, None, :]
    lw = dict(w_in=w_in, w_uq=w_uq_p, w_ukv=w_ukv_p,
              g_mix=g_mix[:, None, :], g_qa=g_qa[:, None, :], g_kva=g_kva[:, None, :],
              qn_a=gain_a(qn_a), kn_a=gain_a(kn_a), qn_b=tile4(qn_b), kn_b=tile4(kn_b),
              qn_d=tile4(qn_d), kn_d=tile4(kn_d))
    return lw, w_o_p


def _constants():
    consts = {}
    ident = lambda cs: (jnp.concatenate([jnp.ones((PREP_BLOCK, LANES), F32), cs[0]]),
                        jnp.concatenate([jnp.zeros((PREP_BLOCK, LANES), F32), cs[1]]))
    consts["cos_a"], consts["sin_a"] = ident(_rope_tables(A_ROPE, LANES, DEC_SEQ))
    consts["cos_b"], consts["sin_b"] = ident(_rope_tables(HEAD_DIM, HEAD_DIM, DEC_SEQ))
    for tag, n in (("ctx", SEQ), ("lat", DEC_SEQ)):
        cl, sl, ccb, scb = _fourier_constants(n)
        consts["cl_" + tag], consts["sl_" + tag], consts["ccb_" + tag], consts["scb_" + tag] = cl, sl, ccb, scb
    consts["ones128"] = _group_ones(512, LANES)
    consts["ones64"] = _group_ones(256, HEAD_DIM)
    return consts


def kernel(x_prompt, x_sample, cache_mla_ckv, cache_mla_krope, cache_win_k, cache_win_v, cache_na_k, cache_na_v, c, c_ctx, w_ada, b_ada, g_ffn1, w_gate1, w_up1, w_down1, g_mix, w_in, g_qa, w_uq, g_kva, w_ukv, qn_a, kn_a, qn_b, kn_b, sink_b, qn_d, kn_d, rpb_d, w_o, g_ffn2, w_gate2, w_up2, w_down2):
    batch, seq, _ = x_prompt.shape
    dec_batch, dec_seq, _ = x_sample.shape
    assert (seq, dec_seq, dec_batch + 1 <= MOD_ROWS) == (SEQ, DEC_SEQ, True)

    consts = _constants()
    lw, w_o_p = _layer_weights(w_in, w_uq, w_ukv, w_o, g_mix, g_qa, g_kva, qn_a, kn_a, qn_b, kn_b, qn_d, kn_d)

    cond = jnp.concatenate([c_ctx[None, :], c, jnp.zeros((MOD_ROWS - 1 - dec_batch, D_MODEL), F32)], axis=0)
    mod = _ada_call(cond, w_ada, b_ada)
    tt = _rpb_call(rpb_d)

    caches = [cache_mla_ckv,
              jnp.pad(cache_mla_krope, ((0, 0), (0, 0), (0, 0), (0, LANES - A_ROPE))),
              cache_win_k.reshape(dec_batch, DEPTH, PAST_LEN, 128),
              cache_win_v.reshape(dec_batch, DEPTH, PAST_LEN, 128),
              cache_na_k.reshape(dec_batch, DEPTH, PAST_LEN, 256),
              cache_na_v.reshape(dec_batch, DEPTH, PAST_LEN, 256)]

    n_prompt = batch * seq
    x = (x_prompt.reshape(n_prompt, D_MODEL), x_sample.reshape(dec_batch * dec_seq, D_MODEL))
    new = [jnp.zeros((batch, DEPTH, seq, w), F32) for w in _CACHE_WIDTHS]
    for l in range(DEPTH):
        sink = sink_b[l]
        x = _ffn_call(x, mod, l, 0, g_ffn1, w_gate1, w_up1, w_down1, n_prompt=n_prompt)
        feat, new = _prep_call(x, mod, l, lw, consts, new, n_prompt=n_prompt)
        mix = (_attn_ctx_call(feat, sink, consts, n_prompt=n_prompt),
               _attn_lat_call(feat, sink, l, caches, lw, consts, tt, n_prompt=n_prompt))
        x = _ffn_call(x, mod, l, 2, g_ffn2, w_gate2, w_up2, w_down2, n_prompt=n_prompt, mix=mix, w_o=w_o_p,
                      split_out=l == DEPTH - 1)

    heads = lambda a, n: a.reshape(batch, DEPTH, seq, n, HEAD_DIM)
    return (x[0].reshape(batch, seq, D_MODEL), x[1].reshape(dec_batch, dec_seq, D_MODEL),
            new[0], new[1], heads(new[2], 2), heads(new[3], 2), heads(new[4], N_HEADS), heads(new[5], N_HEADS))
```

```python
import functools

import numpy as np
import jax
import jax.numpy as jnp
from jax import lax
from jax.experimental import pallas as pl
from jax.experimental.pallas import tpu as pltpu

F32 = jnp.float32
BF16 = jnp.bfloat16

D_MODEL = 1024
DEPTH = 4
SEQ = 256
DEC_SEQ = 1024
PAST_LEN = 256
GRID_W = 64
D_FF = 2816
N_MOD = 9
EPS = 1e-6
ROPE_BASE = 10000.0
A_QK = 96
A_ROPE = 32
B_WINDOW = 128
NA_ROWS = 8
NA_COLS = 16
HEAD_DIM = 64
N_HEADS = 4
NEG = -1e30

LANES = 128
MOD_ROWS = 8
VMEM_LIMIT = 56 * 1024 * 1024

QA, KA, VA = 0, 512, 1024
QB, KB, VB = 1280, 1536, 1664
XC = 1792
QD, KD, VD = 2048, 2304, 2560
FEAT = 2816
B_ORDER = (0, 2, 1, 3)

W_IN_KR, W_IN_QB, W_IN_KB, W_IN_WIDTH = 384, 416, 672, 1952
P_CQ, P_CKV, P_KR, P_QB, P_KB, P_VB, P_XC, P_QD, P_KD, P_VD = (
    0, 256, 384, 512, 768, 896, 1024, 1280, 1536, 1792)
P_WIDTH = 2048

TOK_BLOCK = 512
PREP_BLOCK = 256
FF_CHUNK = 256
N_FF_CHUNKS = D_FF // FF_CHUNK
Q_BLOCK = 256
B_KEYS = 512
D_KEY_ROWS = 12


def _cparams(sem):
    return pltpu.CompilerParams(dimension_semantics=sem, vmem_limit_bytes=VMEM_LIMIT)


def _dot(a, b):
    return jnp.dot(a, b, preferred_element_type=F32)


def _dot_t(a, b):
    return lax.dot_general(a, b, (((1,), (1,)), ((), ())), preferred_element_type=F32)


def _group_sums(sq, ones_bf16):
    return _dot(sq.astype(BF16), ones_bf16)


def _rope_tables(width, slot, n_pos):
    half = width // 2
    quarter = half // 2
    inv = (np.float32(ROPE_BASE) ** (-np.arange(0, half, 2, dtype=np.float32) / np.float32(half))).astype(np.float32)
    t = np.arange(n_pos)
    pos = np.stack([t // GRID_W, t % GRID_W], axis=0).astype(np.float32)
    cos = np.ones((n_pos, LANES), np.float32)
    sin = np.zeros((n_pos, LANES), np.float32)
    for lane in range(LANES):
        i = lane % slot
        if i >= width:
            continue
        axis = i // half
        j = i % half
        ang = pos[axis] * inv[j % quarter]
        cos[:, lane] = np.cos(ang)
        sin[:, lane] = np.sin(ang) * (-1.0 if j < quarter else 1.0)
    return jnp.asarray(cos), jnp.asarray(sin)


def _dft_tables(n):
    k = np.arange(n)
    ang = 2.0 * np.pi * ((k[:, None] * k[None, :]) % n) / n
    return np.cos(ang), np.sin(ang)


def _fourier_constants(n_pos):
    cl, sl = _dft_tables(n_pos)
    cc, sc = _dft_tables(HEAD_DIM)
    scale = 1.0 / np.sqrt(float(HEAD_DIM * n_pos))
    eye = np.eye(N_HEADS)
    ccb = np.kron(eye, cc) * scale
    scb = np.kron(eye, sc) * scale
    as_bf = lambda a: jnp.asarray(a.astype(np.float32)).astype(BF16)
    return as_bf(cl), as_bf(sl), as_bf(ccb), as_bf(scb)


def _group_ones(width, group):
    g = np.arange(width) // group
    return jnp.asarray((g[:, None] == g[None, :]).astype(np.float32)).astype(BF16)


def _ada_kernel(cond_ref, w_ref, b_ref, o_ref):
    c = cond_ref[...]
    s = (c * jax.nn.sigmoid(c)).astype(BF16)
    o_ref[...] = _dot(s, w_ref[...].astype(BF16)) + b_ref[...]


def _ada_call(cond, w_ada, b_ada):
    tn = 1024
    n = N_MOD * D_MODEL
    return pl.pallas_call(
        _ada_kernel,
        grid=(DEPTH, n // tn),
        in_specs=[
            pl.BlockSpec((MOD_ROWS, D_MODEL), lambda l, j: (0, 0)),
            pl.BlockSpec((None, D_MODEL, tn), lambda l, j: (l, 0, j)),
            pl.BlockSpec((None, 1, tn), lambda l, j: (l, 0, j)),
        ],
        out_specs=pl.BlockSpec((None, MOD_ROWS, tn), lambda l, j: (l, 0, j)),
        out_shape=jax.ShapeDtypeStruct((DEPTH, MOD_ROWS, n), F32),
        compiler_params=_cparams(("parallel", "parallel")),
        name="ada_mod",
    )(cond, w_ada, b_ada.reshape(DEPTH, 1, n))


def _rpb_kernel(rpb_ref, o_ref):
    g = pl.program_id(0)
    n_dr = 2 * NA_ROWS - 1
    n_dc = 2 * NA_COLS - 1
    base = g * (n_dr * n_dc)
    shape = (GRID_W, LANES)
    q = lax.broadcasted_iota(jnp.int32, shape, 0)
    lane = lax.broadcasted_iota(jnp.int32, shape, 1)
    w = lane & (GRID_W - 1)
    dc = jnp.clip(w - q + (NA_COLS - 1), 0, n_dc - 1)
    cs = jnp.clip(q - NA_COLS // 2, 0, GRID_W - NA_COLS)
    col_ok = (w >= cs) & (w < cs + NA_COLS)
    neg = jnp.full(shape, NEG, F32)
    tiles = []
    for d in range(n_dr):
        acc = jnp.zeros(shape, F32)
        for j in range(n_dc):
            acc = jnp.where(dc == j, rpb_ref[base + d * n_dc + j], acc)
        tiles.append(jnp.where(col_ok, acc, neg))
    first = lane < GRID_W
    for e in range(n_dr + 1):
        left = tiles[e - 1] if e >= 1 else neg
        right = tiles[e] if e < n_dr else neg
        o_ref[e] = jnp.where(first, left, right)


def _rpb_call(rpb_d):
    n_e = 2 * NA_ROWS
    return pl.pallas_call(
        _rpb_kernel,
        grid=(DEPTH * N_HEADS,),
        in_specs=[pl.BlockSpec(memory_space=pltpu.SMEM)],
        out_specs=pl.BlockSpec((None, n_e, GRID_W, LANES), lambda g: (g, 0, 0, 0)),
        out_shape=jax.ShapeDtypeStruct((DEPTH * N_HEADS, n_e, GRID_W, LANES), F32),
        compiler_params=_cparams(("parallel",)),
        name="rpb_tiles",
    )(rpb_d.reshape(-1))


def _mod_row(b, n_prompt_blocks, blocks_per_batch):
    return jnp.where(b < n_prompt_blocks, 0, 1 + (b - n_prompt_blocks) // blocks_per_batch)


def _ffn_kernel(*refs, layer, n_prompt_blocks, blocks_per_batch, with_mix, split_in, split_out):
    n_x = 2 if split_in else 1
    x_refs, refs = refs[:n_x], refs[n_x:]
    if with_mix:
        mixp_ref, mixs_ref, wo_ref, g2_ref = refs[:4]
        refs = refs[4:]
    sh_ref, sc_ref, g_ref, gn_ref, wg_hbm, wu_hbm, wd_hbm = refs[:7]
    n_o = 2 if split_out else 1
    o_refs = refs[7:7 + n_o]
    wg_scr, wu_scr, wd_scr, stg_g, stg_u, stg_d, sem, x_scr, a_scr = refs[7 + n_o:]
    b = pl.program_id(0)
    is_prompt = b < n_prompt_blocks
    row = _mod_row(b, n_prompt_blocks, blocks_per_batch)

    def weight_copies(c, slot):
        cols = pl.ds(c * FF_CHUNK, FF_CHUNK)
        return (pltpu.make_async_copy(wg_hbm.at[layer, :, cols], stg_g.at[slot], sem.at[0, slot]),
                pltpu.make_async_copy(wu_hbm.at[layer, :, cols], stg_u.at[slot], sem.at[1, slot]),
                pltpu.make_async_copy(wd_hbm.at[layer, cols, :], stg_d.at[slot], sem.at[2, slot]))

    def block(first):
        if first:
            for slot in range(2):
                for cp in weight_copies(slot, slot):
                    cp.start()
        x = jnp.where(is_prompt, x_refs[0][...], x_refs[1][...]) if split_in else x_refs[0][...]
        if with_mix:
            mix = jnp.where(is_prompt, mixp_ref[...], mixs_ref[...])
            x = x + g2_ref[pl.ds(row, 1), :] * _dot(mix, wo_ref[...])
        x_scr[...] = x
        y = x * lax.rsqrt(jnp.mean(x * x, axis=-1, keepdims=True) + EPS) * gn_ref[...]
        h = (y * (1.0 + sc_ref[pl.ds(row, 1), :]) + sh_ref[pl.ds(row, 1), :]).astype(BF16)
        for c in range(N_FF_CHUNKS):
            lanes = slice(c * FF_CHUNK, (c + 1) * FF_CHUNK)
            if first:
                slot = c % 2
                for cp in weight_copies(c, slot):
                    cp.wait()
                wg_scr[:, lanes] = stg_g[slot].astype(BF16)
                wu_scr[:, lanes] = stg_u[slot].astype(BF16)
                wd_scr[lanes, :] = stg_d[slot].astype(BF16)
                if c + 2 < N_FF_CHUNKS:
                    for cp in weight_copies(c + 2, slot):
                        cp.start()
            gate = _dot(h, wg_scr[:, lanes])
            up = _dot(h, wu_scr[:, lanes])
            a_scr[:, lanes] = ((gate * jax.nn.sigmoid(gate)) * up).astype(BF16)
        out = x_scr[...] + (0.5 * g_ref[pl.ds(row, 1), :]) * _dot(a_scr[...], wd_scr[...])
        if not split_out:
            o_refs[0][...] = out
        elif first:
            o_refs[0][...] = out
        else:
            @pl.when(is_prompt)
            def _():
                o_refs[0][...] = out

            @pl.when(jnp.logical_not(is_prompt))
            def _():
                o_refs[1][...] = out

    pl.when(b == 0)(functools.partial(block, True))
    pl.when(b != 0)(functools.partial(block, False))


def _ffn_call(x, mod, layer, slot, g_norm, w_gate, w_up, w_down, *, n_prompt, mix=None, w_o=None, split_out=False):
    split_in = isinstance(x, tuple)
    t = x[0].shape[0] + x[1].shape[0] if split_in else x.shape[0]
    tm = TOK_BLOCK
    npb = n_prompt // tm
    with_mix = mix is not None
    mod_spec = lambda k: pl.BlockSpec((None, MOD_ROWS, D_MODEL), lambda b: (layer, 0, k))
    x_spec = pl.BlockSpec((tm, D_MODEL), lambda b: (b, 0))
    p_spec = pl.BlockSpec((tm, D_MODEL), lambda b: (jnp.minimum(b, npb - 1), 0))
    s_spec = pl.BlockSpec((tm, D_MODEL), lambda b: (jnp.maximum(b - npb, 0), 0))
    hbm = pl.BlockSpec(memory_space=pl.ANY)
    in_specs = [p_spec, s_spec] if split_in else [x_spec]
    args = list(x) if split_in else [x]
    if with_mix:
        in_specs += [p_spec, s_spec, pl.BlockSpec((None, D_MODEL, D_MODEL), lambda b: (layer, 0, 0)), mod_spec(5)]
        args += [mix[0], mix[1], w_o, mod]
    c0 = 3 * slot
    in_specs += [mod_spec(c0), mod_spec(c0 + 1), mod_spec(c0 + 2),
                 pl.BlockSpec((None, 1, D_MODEL), lambda b: (layer, 0, 0)), hbm, hbm, hbm]
    args += [mod, mod, mod, g_norm.reshape(DEPTH, 1, D_MODEL), w_gate, w_up, w_down]
    scratch = [pltpu.VMEM((D_MODEL, D_FF), BF16), pltpu.VMEM((D_MODEL, D_FF), BF16), pltpu.VMEM((D_FF, D_MODEL), BF16),
               pltpu.VMEM((2, D_MODEL, FF_CHUNK), F32), pltpu.VMEM((2, D_MODEL, FF_CHUNK), F32),
               pltpu.VMEM((2, FF_CHUNK, D_MODEL), F32), pltpu.SemaphoreType.DMA((3, 2)),
               pltpu.VMEM((tm, D_MODEL), F32), pltpu.VMEM((tm, D_FF), BF16)]
    if split_out:
        out_specs = [p_spec, s_spec]
        out_shape = [jax.ShapeDtypeStruct((n_prompt, D_MODEL), F32), jax.ShapeDtypeStruct((t - n_prompt, D_MODEL), F32)]
    else:
        out_specs = x_spec
        out_shape = jax.ShapeDtypeStruct((t, D_MODEL), F32)
    return pl.pallas_call(
        functools.partial(_ffn_kernel, layer=layer, n_prompt_blocks=npb, blocks_per_batch=DEC_SEQ // tm,
                          with_mix=with_mix, split_in=split_in, split_out=split_out),
        grid=(t // tm,),
        in_specs=in_specs,
        out_specs=out_specs,
        out_shape=out_shape,
        scratch_shapes=scratch,
        compiler_params=_cparams(("arbitrary",)),
        name="ffn_mix" if with_mix else "ffn",
    )(*args)


def _swap(x, width):
    lane = lax.broadcasted_iota(jnp.int32, x.shape, 1)
    first = (lane & width) == 0
    return jnp.where(first, pltpu.roll(x, LANES - width, 1), pltpu.roll(x, width, 1))


def _rope(x, cos, sin, width):
    parts = []
    for c in range(x.shape[1] // LANES):
        xc = x[:, c * LANES:(c + 1) * LANES]
        parts.append(xc * cos + _swap(xc, width) * sin)
    return parts[0] if len(parts) == 1 else jnp.concatenate(parts, axis=1)


def _head_norm(x, ones, gain, inv_dim):
    ms = _group_sums(x * x, ones) * inv_dim
    return x * lax.rsqrt(ms + EPS) * gain


def _mla_keys(ckv_n_bf16, kr_pad, w_ukv, ones128, kn_a):
    kv = _dot(ckv_n_bf16, w_ukv)
    k = kv[:, :512] + jnp.concatenate([kr_pad] * N_HEADS, axis=1)
    return _head_norm(k, ones128, kn_a, 1.0 / A_QK), kv[:, 512:]


def _prep_kernel(*refs, n_prompt_blocks, blocks_per_batch, n_alias):
    (x_ref, sh_ref, sc_ref, gn_ref, win_ref, gqa_ref, wuq_ref, gkva_ref, wukv_ref,
     qna_ref, kna_ref, qnb_ref, knb_ref, qnd_ref, knd_ref,
     ones128_ref, ones64_ref, cos_a_ref, sin_a_ref, cos_b_ref, sin_b_ref) = refs[:21]
    feat_ref, ckv_ref, kr_ref, wk_ref, wv_ref, nk_ref, nv_ref, win_scr = refs[21 + n_alias:]
    b = pl.program_id(0)
    row = _mod_row(b, n_prompt_blocks, blocks_per_batch)
    is_prompt = b < n_prompt_blocks

    @pl.when(b == 0)
    def _():
        src_q = W_IN_QB
        for r in range(0, D_MODEL, PREP_BLOCK):
            rows = slice(r, r + PREP_BLOCK)
            win_scr[rows, 0:P_KR + A_ROPE] = win_ref[rows, 0:W_IN_KR + A_ROPE].astype(BF16)
            win_scr[rows, P_KR + A_ROPE:P_QB] = jnp.zeros((PREP_BLOCK, LANES - A_ROPE), BF16)
            for p, hd in enumerate(B_ORDER):
                win_scr[rows, P_QB + HEAD_DIM * p:P_QB + HEAD_DIM * (p + 1)] = (
                    win_ref[rows, src_q + HEAD_DIM * hd:src_q + HEAD_DIM * (hd + 1)].astype(BF16))
            win_scr[rows, P_KB:P_WIDTH] = win_ref[rows, W_IN_KB:W_IN_WIDTH].astype(BF16)

    x = x_ref[...]
    y = x * lax.rsqrt(jnp.mean(x * x, axis=-1, keepdims=True) + EPS) * gn_ref[...]
    h = (y * (1.0 + sc_ref[pl.ds(row, 1), :]) + sh_ref[pl.ds(row, 1), :]).astype(BF16)
    proj = _dot(h, win_scr[...])
    ones128 = ones128_ref[...]
    ones64 = ones64_ref[...]

    cq = proj[:, P_CQ:P_CQ + 256]
    cqn = cq * lax.rsqrt(jnp.mean(cq * cq, axis=-1, keepdims=True) + EPS) * gqa_ref[...]
    qa = _head_norm(_dot(cqn.astype(BF16), wuq_ref[...]), ones128, qna_ref[...], 1.0 / A_QK)
    ckv = proj[:, P_CKV:P_CKV + 128]
    ckv_n = ckv * lax.rsqrt(jnp.mean(ckv * ckv, axis=-1, keepdims=True) + EPS) * gkva_ref[...]
    kr_pad = proj[:, P_KR:P_KR + 128]
    ka, va = _mla_keys(ckv_n.astype(BF16), kr_pad, wukv_ref[...], ones128, kna_ref[...])
    cos_a, sin_a = cos_a_ref[...], sin_a_ref[...]
    qa = _rope(qa, cos_a, sin_a, 8)
    ka = _rope(ka, cos_a, sin_a, 8)
    feat_ref[:, QA:QA + 512] = (qa * (A_QK ** -0.5)).astype(BF16)
    feat_ref[:, KA:KA + 512] = ka.astype(BF16)
    feat_ref[:, VA:VA + 256] = va.astype(BF16)

    qb = _head_norm(proj[:, P_QB:P_QB + 256], ones64, qnb_ref[...], 1.0 / HEAD_DIM)
    kb = _head_norm(proj[:, P_KB:P_KB + 128], ones64[:128, :128], knb_ref[:, :128], 1.0 / HEAD_DIM)
    vb = proj[:, P_VB:P_VB + 128]
    cos_b, sin_b = cos_b_ref[...], sin_b_ref[...]
    qb = _rope(qb, cos_b, sin_b, 16)
    kb = _rope(kb, cos_b, sin_b, 16)
    feat_ref[:, QB:QB + 256] = (qb * (HEAD_DIM ** -0.5)).astype(BF16)
    feat_ref[:, KB:KB + 128] = kb.astype(BF16)
    feat_ref[:, VB:VB + 128] = vb.astype(BF16)

    feat_ref[:, XC:XC + 256] = proj[:, P_XC:P_XC + 256].astype(BF16)

    qd = _head_norm(proj[:, P_QD:P_QD + 256], ones64, qnd_ref[...], 1.0 / HEAD_DIM)
    kd = _head_norm(proj[:, P_KD:P_KD + 256], ones64, knd_ref[...], 1.0 / HEAD_DIM)
    vd = proj[:, P_VD:P_VD + 256]
    feat_ref[:, QD:QD + 256] = (qd * (HEAD_DIM ** -0.5)).astype(BF16)
    feat_ref[:, KD:KD + 256] = kd.astype(BF16)
    feat_ref[:, VD:VD + 256] = vd.astype(BF16)

    @pl.when(is_prompt)
    def _():
        ckv_ref[...] = ckv_n
        kr_ref[...] = kr_pad[:, :A_ROPE]
        wk_ref[...] = kb
        wv_ref[...] = vb
        nk_ref[...] = kd
        nv_ref[...] = vd


_PREP_WEIGHTS = ("g_mix", "w_in", "g_qa", "w_uq", "g_kva", "w_ukv",
                 "qn_a", "kn_a", "qn_b", "kn_b", "qn_d", "kn_d")
_CACHE_WIDTHS = (128, A_ROPE, 128, 128, 256, 256)


def _prep_call(x, mod, layer, lw, consts, new_caches, *, n_prompt):
    t = x.shape[0]
    tm = PREP_BLOCK
    npb = n_prompt // tm
    bpb = DEC_SEQ // tm
    batch = n_prompt // SEQ
    full = lambda a: pl.BlockSpec(a.shape, lambda b: (0,) * a.ndim)
    mod_spec = lambda k: pl.BlockSpec((None, MOD_ROWS, D_MODEL), lambda b: (layer, 0, k))
    lyr = lambda a: pl.BlockSpec((None,) + a.shape[1:], lambda b: (layer,) + (0,) * (a.ndim - 1))
    tok = lambda w: pl.BlockSpec((tm, w), lambda b: (b, 0))
    pos = pl.BlockSpec((tm, LANES), lambda b: (jnp.where(b < npb, 0, 1 + (b - npb) % bpb), 0))
    weights = [lw[k] for k in _PREP_WEIGHTS]
    tabs = [consts["cos_a"], consts["sin_a"], consts["cos_b"], consts["sin_b"]]
    in_specs = [tok(D_MODEL), mod_spec(3), mod_spec(4)] + [lyr(a) for a in weights]
    in_specs += [full(consts["ones128"]), full(consts["ones64"])] + [pos] * len(tabs)
    args = [x, mod, mod] + weights + [consts["ones128"], consts["ones64"]] + tabs
    aliases = {len(args) + k: 1 + k for k in range(len(new_caches))}
    in_specs += [pl.BlockSpec(memory_space=pl.ANY)] * len(new_caches)
    args += list(new_caches)
    spb = SEQ // tm
    cache_spec = lambda w: pl.BlockSpec(
        (None, None, tm, w), lambda b: (jnp.minimum(b, npb - 1) // spb, layer, jnp.minimum(b, npb - 1) % spb, 0))
    out_specs = [tok(FEAT)] + [cache_spec(w) for w in _CACHE_WIDTHS]
    out_shape = [jax.ShapeDtypeStruct((t, FEAT), BF16)]
    out_shape += [jax.ShapeDtypeStruct((batch, DEPTH, SEQ, w), F32) for w in _CACHE_WIDTHS]
    feat, *caches = pl.pallas_call(
        functools.partial(_prep_kernel, n_prompt_blocks=npb, blocks_per_batch=bpb, n_alias=len(aliases)),
        grid=(t // tm,),
        in_specs=in_specs,
        out_specs=out_specs,
        out_shape=out_shape,
        input_output_aliases=aliases,
        scratch_shapes=[pltpu.VMEM((D_MODEL, P_WIDTH), BF16)],
        compiler_params=_cparams(("arbitrary",)),
        name="prep",
    )(*args)
    return feat, caches


def _slot_mask(shape, width, slot):
    lane = lax.broadcasted_iota(jnp.int32, shape, 1)
    return (lane >= slot * width) & (lane < (slot + 1) * width)


def _softmax_pv(scores, values, slot, sink=None):
    m = functools.reduce(jnp.maximum, [jnp.max(s, axis=-1, keepdims=True) for s in scores])
    if sink is not None:
        m = jnp.maximum(m, sink)
    one = jnp.ones((), BF16)
    o = functools.reduce(jnp.add, [
        _dot(jnp.exp(s - m).astype(BF16), jnp.where(_slot_mask(v.shape, HEAD_DIM, slot), v, one))
        for s, v in zip(scores, values)])
    groups = [pltpu.roll(o[:, c * LANES:(c + 1) * LANES], HEAD_DIM, 1) for c in range(o.shape[1] // LANES)]
    den = groups[0] if len(groups) == 1 else jnp.concatenate(groups, axis=1)
    if sink is not None:
        den = den + jnp.exp(sink - m)
    return o / den


def _attn_ctx_kernel(sink_ref, feat_ref, cl_ref, sl_ref, ccb_ref, scb_ref, o_ref):
    n = feat_ref.shape[0]
    zero = jnp.zeros((), BF16)

    va = feat_ref[:, VA:VA + 256]
    oa = jnp.zeros((n, 256), F32)
    for h in range(N_HEADS):
        q = feat_ref[:, QA + 128 * h:QA + 128 * (h + 1)]
        k = feat_ref[:, KA + 128 * h:KA + 128 * (h + 1)]
        o = _softmax_pv([_dot_t(q, k)], [va], h)
        oa = jnp.where(_slot_mask(o.shape, HEAD_DIM, h), o, oa)
    o_ref[:, 0:256] = oa.astype(BF16)

    kb = feat_ref[:, KB:KB + 128]
    vb = feat_ref[:, VB:VB + 128]
    for c in range(2):
        qc = feat_ref[:, QB + 128 * c:QB + 128 * (c + 1)]
        ob = jnp.zeros((n, 128), F32)
        for half in range(2):
            head = B_ORDER[2 * c + half]
            msk = _slot_mask(qc.shape, HEAD_DIM, half)
            o = _softmax_pv([_dot_t(jnp.where(msk, qc, zero), kb)], [vb], half, sink=sink_ref[head])
            ob = jnp.where(msk, o, ob)
        o_ref[:, 256 + 128 * c:256 + 128 * (c + 1)] = ob.astype(BF16)

    xc = feat_ref[:, XC:XC + 256]
    fa = _dot(xc, ccb_ref[...]).astype(BF16)
    fb = _dot(xc, scb_ref[...]).astype(BF16)
    o_ref[:, 512:768] = (_dot(cl_ref[...], fa) - _dot(sl_ref[...], fb)).astype(BF16)

    qd = feat_ref[:, QD:QD + 256]
    kd = feat_ref[:, KD:KD + 256]
    vd = feat_ref[:, VD:VD + 256]
    od = jnp.zeros((n, 256), F32)
    for h in range(N_HEADS):
        msk = _slot_mask(qd.shape, HEAD_DIM, h)
        o = _softmax_pv([_dot_t(jnp.where(msk, qd, zero), kd)], [vd], h)
        od = jnp.where(msk, o, od)
    o_ref[:, 768:1024] = od.astype(BF16)


def _attn_ctx_call(feat, sink, consts, *, n_prompt):
    t = n_prompt
    full = lambda a: pl.BlockSpec(a.shape, lambda i: (0,) * a.ndim)
    tabs = [consts["cl_ctx"], consts["sl_ctx"], consts["ccb_ctx"], consts["scb_ctx"]]
    return pl.pallas_call(
        _attn_ctx_kernel,
        grid=(t // SEQ,),
        in_specs=[pl.BlockSpec(memory_space=pltpu.SMEM),
                  pl.BlockSpec((SEQ, FEAT), lambda i: (i, 0))] + [full(a) for a in tabs],
        out_specs=pl.BlockSpec((SEQ, D_MODEL), lambda i: (i, 0)),
        out_shape=jax.ShapeDtypeStruct((t, D_MODEL), BF16),
        compiler_params=_cparams(("parallel",)),
        name="attn_ctx",
    )(sink, feat, *tabs)


def _attn_lat_kernel(sink_ref, feat_ref, cckv_ref, ckr_ref, cwk_ref, cwv_ref, cnk_ref, cnv_ref,
                     wukv_ref, kna_ref, ones128_ref, tt_ref, cl_ref, sl_ref, ccb_ref, scb_ref,
                     o_ref, kac_scr, vac_scr, fa_scr, fb_scr):
    j = pl.program_id(1)
    nq = Q_BLOCK
    zero = jnp.zeros((), BF16)

    @pl.when(j == 0)
    def _():
        ka_c, va_c = _mla_keys(cckv_ref[...].astype(BF16), ckr_ref[...], wukv_ref[...],
                               ones128_ref[...], kna_ref[...])
        kac_scr[...] = ka_c.astype(BF16)
        vac_scr[...] = va_c.astype(BF16)
        xc = feat_ref[:, XC:XC + 256]
        fa_scr[...] = _dot(xc, ccb_ref[...]).astype(BF16)
        fb_scr[...] = _dot(xc, scb_ref[...]).astype(BF16)

    r0 = pl.multiple_of(j * nq, nq)
    rows = pl.ds(r0, nq)

    va = feat_ref[:, VA:VA + 256]
    va_c = vac_scr[...]
    oa = jnp.zeros((nq, 256), F32)
    for h in range(N_HEADS):
        q = feat_ref[rows, QA + 128 * h:QA + 128 * (h + 1)]
        s_c = _dot_t(q, kac_scr[:, 128 * h:128 * (h + 1)])
        s_l = _dot_t(q, feat_ref[:, KA + 128 * h:KA + 128 * (h + 1)])
        o = _softmax_pv([s_c, s_l], [va_c, va], h)
        oa = jnp.where(_slot_mask(o.shape, HEAD_DIM, h), o, oa)
    o_ref[:, 0:256] = oa.astype(BF16)

    start = pl.multiple_of(jnp.clip(r0 - B_WINDOW, 0, DEC_SEQ - B_KEYS), B_WINDOW)
    kw = feat_ref[pl.ds(start, B_KEYS), KB:KB + 128]
    vw = feat_ref[pl.ds(start, B_KEYS), VB:VB + 128]
    kb_c = cwk_ref[...].astype(BF16)
    vb_c = cwv_ref[...].astype(BF16)
    qpos = r0 + lax.broadcasted_iota(jnp.int32, (nq, B_KEYS), 0)
    kpos = start + lax.broadcasted_iota(jnp.int32, (nq, B_KEYS), 1)
    band = jnp.abs(qpos - kpos) <= B_WINDOW
    for c in range(2):
        qc = feat_ref[rows, QB + 128 * c:QB + 128 * (c + 1)]
        ob = jnp.zeros((nq, 128), F32)
        for half in range(2):
            head = B_ORDER[2 * c + half]
            msk = _slot_mask(qc.shape, HEAD_DIM, half)
            q = jnp.where(msk, qc, zero)
            s_l = jnp.where(band, _dot_t(q, kw), NEG)
            s_c = _dot_t(q, kb_c)
            o = _softmax_pv([s_l, s_c], [vw, vb_c], half, sink=sink_ref[head])
            ob = jnp.where(msk, o, ob)
        o_ref[:, 256 + 128 * c:256 + 128 * (c + 1)] = ob.astype(BF16)

    o_ref[:, 512:768] = (_dot(cl_ref[rows, :], fa_scr[...]) - _dot(sl_ref[rows, :], fb_scr[...])).astype(BF16)

    rows_per_block = nq // GRID_W
    n_rows = DEC_SEQ // GRID_W
    kr_win = min(NA_ROWS, n_rows)
    ks = jnp.where(j < 2, 0, n_rows - D_KEY_ROWS)
    k0_tok = pl.multiple_of(ks * GRID_W, GRID_W)
    keys = pl.ds(k0_tok, D_KEY_ROWS * GRID_W)
    kdw = feat_ref[keys, KD:KD + 256]
    vdw = feat_ref[keys, VD:VD + 256]
    kd_c = cnk_ref[...].astype(BF16)
    vd_c = cnv_ref[...].astype(BF16)
    qd = feat_ref[rows, QD:QD + 256]
    first = lax.broadcasted_iota(jnp.int32, (GRID_W, LANES), 1) < GRID_W
    od = jnp.zeros((nq, 256), F32)
    for h in range(N_HEADS):
        bias_rows = []
        for rr in range(rows_per_block):
            r = j * rows_per_block + rr
            rs = jnp.clip(r - kr_win // 2, 0, n_rows - kr_win)
            tiles = []
            for p in range(D_KEY_ROWS // 2):
                ka0 = ks + 2 * p
                e = jnp.clip(ka0 - r + NA_ROWS, 0, 2 * NA_ROWS - 1)
                ok0 = ((ka0 >= rs) & (ka0 < rs + kr_win)).astype(jnp.int32)
                ok1 = ((ka0 + 1 >= rs) & (ka0 + 1 < rs + kr_win)).astype(jnp.int32)
                ok = jnp.where(first, ok0, ok1) > 0
                tiles.append(jnp.where(ok, tt_ref[h, e], NEG))
            bias_rows.append(jnp.concatenate(tiles, axis=1))
        bias = jnp.concatenate(bias_rows, axis=0)
        msk = _slot_mask(qd.shape, HEAD_DIM, h)
        q = jnp.where(msk, qd, zero)
        s_l = _dot_t(q, kdw) + bias
        s_c = _dot_t(q, kd_c)
        o = _softmax_pv([s_l, s_c], [vdw, vd_c], h)
        od = jnp.where(msk, o, od)
    o_ref[:, 768:1024] = od.astype(BF16)


def _attn_lat_call(feat, sink, layer, caches, lw, consts, tt, *, n_prompt):
    first = n_prompt // DEC_SEQ
    nb = feat.shape[0] // DEC_SEQ - first
    full = lambda a: pl.BlockSpec(a.shape, lambda b, j: (0,) * a.ndim)
    lyr = lambda a: pl.BlockSpec((None,) + a.shape[1:], lambda b, j: (layer,) + (0,) * (a.ndim - 1))
    cache = lambda a: pl.BlockSpec((None, None) + a.shape[2:], lambda b, j: (b, layer, 0, 0))
    tabs = [consts["cl_lat"], consts["sl_lat"], consts["ccb_lat"], consts["scb_lat"]]
    in_specs = [pl.BlockSpec(memory_space=pltpu.SMEM),
                pl.BlockSpec((DEC_SEQ, FEAT), lambda b, j: (first + b, 0))]
    in_specs += [cache(a) for a in caches]
    in_specs += [lyr(lw["w_ukv"]), lyr(lw["kn_a"]), full(consts["ones128"]),
                 pl.BlockSpec((N_HEADS,) + tt.shape[1:], lambda b, j: (layer, 0, 0, 0))]
    in_specs += [full(a) for a in tabs]
    return pl.pallas_call(
        _attn_lat_kernel,
        grid=(nb, DEC_SEQ // Q_BLOCK),
        in_specs=in_specs,
        out_specs=pl.BlockSpec((Q_BLOCK, D_MODEL), lambda b, j: (b * (DEC_SEQ // Q_BLOCK) + j, 0)),
        out_shape=jax.ShapeDtypeStruct((nb * DEC_SEQ, D_MODEL), BF16),
        scratch_shapes=[pltpu.VMEM((PAST_LEN, 512), BF16), pltpu.VMEM((PAST_LEN, 256), BF16),
                        pltpu.VMEM((DEC_SEQ, 256), BF16), pltpu.VMEM((DEC_SEQ, 256), BF16)],
        compiler_params=_cparams(("parallel", "arbitrary")),
        name="attn_lat",
    )(sink, feat, *caches, lw["w_ukv"], lw["kn_a"], consts["ones128"], tt, *tabs)


def _layer_weights(w_in, w_uq, w_ukv, w_o, g_mix, g_qa, g_kva, qn_a, kn_a, qn_b, kn_b, qn_d, kn_d):
    z = lambda *s: jnp.zeros(s, F32)

    def slots(a, rows):
        parts = []
        for h in range(N_HEADS):
            hd = a[..., A_QK * h:A_QK * (h + 1)]
            parts += [hd[..., HEAD_DIM:], z(*rows, LANES - A_QK), hd[..., :HEAD_DIM]]
        return jnp.concatenate(parts, axis=-1)

    w_uq_p = slots(w_uq, (DEPTH, 256)).astype(BF16)
    k_cols, v_cols = [], []
    for h in range(N_HEADS):
        k_cols += [z(DEPTH, 128, HEAD_DIM), w_ukv[..., 128 * h:128 * h + HEAD_DIM]]
        v_cols.append(w_ukv[..., 128 * h + HEAD_DIM:128 * (h + 1)])
    w_ukv_p = jnp.concatenate(k_cols + v_cols, axis=-1).astype(BF16)
    w_o_p = jnp.concatenate(
        [w_o[:, :256]] + [w_o[:, 256 + HEAD_DIM * h:256 + HEAD_DIM * (h + 1)] for h in B_ORDER]
        + [w_o[:, 512:]], axis=1).astype(BF16)
    gain_a = lambda g: slots(jnp.tile(g, (1, N_HEADS)), (DEPTH,))[:, None, :]
    tile4 = lambda g: jnp.tile(g, (1, N_HEADS))[:, None, :]
    lw = dict(w_in=w_in, w_uq=w_uq_p, w_ukv=w_ukv_p,
              g_mix=g_mix[:, None, :], g_qa=g_qa[:, None, :], g_kva=g_kva[:, None, :],
              qn_a=gain_a(qn_a), kn_a=gain_a(kn_a), qn_b=tile4(qn_b), kn_b=tile4(kn_b),
              qn_d=tile4(qn_d), kn_d=tile4(kn_d))
    return lw, w_o_p


def _constants():
    consts = {}
    ident = lambda cs: (jnp.concatenate([jnp.ones((PREP_BLOCK, LANES), F32), cs[0]]),
                        jnp.concatenate([jnp.zeros((PREP_BLOCK, LANES), F32), cs[1]]))
    consts["cos_a"], consts["sin_a"] = ident(_rope_tables(A_ROPE, LANES, DEC_SEQ))
    consts["cos_b"], consts["sin_b"] = ident(_rope_tables(HEAD_DIM, HEAD_DIM, DEC_SEQ))
    for tag, n in (("ctx", SEQ), ("lat", DEC_SEQ)):
        cl, sl, ccb, scb = _fourier_constants(n)
        consts["cl_" + tag], consts["sl_" + tag], consts["ccb_" + tag], consts["scb_" + tag] = cl, sl, ccb, scb
    consts["ones128"] = _group_ones(512, LANES)
    consts["ones64"] = _group_ones(256, HEAD_DIM)
    return consts


def kernel(x_prompt, x_sample, cache_mla_ckv, cache_mla_krope, cache_win_k, cache_win_v, cache_na_k, cache_na_v, c, c_ctx, w_ada, b_ada, g_ffn1, w_gate1, w_up1, w_down1, g_mix, w_in, g_qa, w_uq, g_kva, w_ukv, qn_a, kn_a, qn_b, kn_b, sink_b, qn_d, kn_d, rpb_d, w_o, g_ffn2, w_gate2, w_up2, w_down2):
    batch, seq, _ = x_prompt.shape
    dec_batch, dec_seq, _ = x_sample.shape
    assert (seq, dec_seq, dec_batch + 1 <= MOD_ROWS) == (SEQ, DEC_SEQ, True)

    consts = _constants()
    lw, w_o_p = _layer_weights(w_in, w_uq, w_ukv, w_o, g_mix, g_qa, g_kva, qn_a, kn_a, qn_b, kn_b, qn_d, kn_d)

    cond = jnp.concatenate([c_ctx[None, :], c, jnp.zeros((MOD_ROWS - 1 - dec_batch, D_MODEL), F32)], axis=0)
    mod = _ada_call(cond, w_ada, b_ada)
    tt = _rpb_call(rpb_d)

    caches = [cache_mla_ckv,
              jnp.pad(cache_mla_krope, ((0, 0), (0, 0), (0, 0), (0, LANES - A_ROPE))),
              cache_win_k.reshape(dec_batch, DEPTH, PAST_LEN, 128),
              cache_win_v.reshape(dec_batch, DEPTH, PAST_LEN, 128),
              cache_na_k.reshape(dec_batch, DEPTH, PAST_LEN, 256),
              cache_na_v.reshape(dec_batch, DEPTH, PAST_LEN, 256)]

    n_prompt = batch * seq
    x = (x_prompt.reshape(n_prompt, D_MODEL), x_sample.reshape(dec_batch * dec_seq, D_MODEL))
    new = [jnp.zeros((batch, DEPTH, seq, w), F32) for w in _CACHE_WIDTHS]
    for l in range(DEPTH):
        sink = sink_b[l]
        x = _ffn_call(x, mod, l, 0, g_ffn1, w_gate1, w_up1, w_down1, n_prompt=n_prompt)
        feat, new = _prep_call(x, mod, l, lw, consts, new, n_prompt=n_prompt)
        mix = (_attn_ctx_call(feat, sink, consts, n_prompt=n_prompt),
               _attn_lat_call(feat, sink, l, caches, lw, consts, tt, n_prompt=n_prompt))
        x = _ffn_call(x, mod, l, 2, g_ffn2, w_gate2, w_up2, w_down2, n_prompt=n_prompt, mix=mix, w_o=w_o_p,
                      split_out=l == DEPTH - 1)

    heads = lambda a, n: a.reshape(batch, DEPTH, seq, n, HEAD_DIM)
    return (x[0].reshape(batch, seq, D_MODEL), x[1].reshape(dec_batch, dec_seq, D_MODEL),
            new[0], new[1], heads(new[2], 2), heads(new[3], 2), heads(new[4], N_HEADS), heads(new[5], N_HEADS))
```

```python
import functools

import numpy as np
import jax
import jax.numpy as jnp
from jax import lax
from jax.experimental import pallas as pl
from jax.experimental.pallas import tpu as pltpu

F32 = jnp.float32
BF16 = jnp.bfloat16

D_MODEL = 1024
DEPTH = 4
SEQ = 256
DEC_SEQ = 1024
PAST_LEN = 256
GRID_W = 64
D_FF = 2816
N_MOD = 9
EPS = 1e-6
ROPE_BASE = 10000.0
A_QK = 96
A_ROPE = 32
B_WINDOW = 128
NA_ROWS = 8
NA_COLS = 16
HEAD_DIM = 64
N_HEADS = 4
NEG = -1e30

LANES = 128
MOD_ROWS = 8
VMEM_LIMIT = 56 * 1024 * 1024

QA, KA, VA = 0, 512, 1024
QB, KB, VB = 1280, 1536, 1664
XC = 1792
QD, KD, VD = 2048, 2304, 2560
FEAT = 2816
B_ORDER = (0, 2, 1, 3)

W_IN_KR, W_IN_QB, W_IN_KB, W_IN_WIDTH = 384, 416, 672, 1952
P_CQ, P_CKV, P_KR, P_QB, P_KB, P_VB, P_XC, P_QD, P_KD, P_VD = (
    0, 256, 384, 512, 768, 896, 1024, 1280, 1536, 1792)
P_WIDTH = 2048

TOK_BLOCK = 512
PREP_BLOCK = 256
FF_CHUNK = 256
N_FF_CHUNKS = D_FF // FF_CHUNK
Q_BLOCK = 256
B_KEYS = 512
D_KEY_ROWS = 12


def _cparams(sem):
    return pltpu.CompilerParams(dimension_semantics=sem, vmem_limit_bytes=VMEM_LIMIT)


def _dot(a, b):
    return jnp.dot(a, b, preferred_element_type=F32)


def _dot_t(a, b):
    return lax.dot_general(a, b, (((1,), (1,)), ((), ())), preferred_element_type=F32)


def _group_sums(sq, ones_bf16):
    return _dot(sq.astype(BF16), ones_bf16)


def _rope_tables(width, slot, n_pos):
    half = width // 2
    quarter = half // 2
    inv = (np.float32(ROPE_BASE) ** (-np.arange(0, half, 2, dtype=np.float32) / np.float32(half))).astype(np.float32)
    t = np.arange(n_pos)
    pos = np.stack([t // GRID_W, t % GRID_W], axis=0).astype(np.float32)
    cos = np.ones((n_pos, LANES), np.float32)
    sin = np.zeros((n_pos, LANES), np.float32)
    for lane in range(LANES):
        i = lane % slot
        if i >= width:
            continue
        axis = i // half
        j = i % half
        ang = pos[axis] * inv[j % quarter]
        cos[:, lane] = np.cos(ang)
        sin[:, lane] = np.sin(ang) * (-1.0 if j < quarter else 1.0)
    return jnp.asarray(cos), jnp.asarray(sin)


def _dft_tables(n):
    k = np.arange(n)
    ang = 2.0 * np.pi * ((k[:, None] * k[None, :]) % n) / n
    return np.cos(ang), np.sin(ang)


def _fourier_constants(n_pos):
    cl, sl = _dft_tables(n_pos)
    cc, sc = _dft_tables(HEAD_DIM)
    scale = 1.0 / np.sqrt(float(HEAD_DIM * n_pos))
    eye = np.eye(N_HEADS)
    ccb = np.kron(eye, cc) * scale
    scb = np.kron(eye, sc) * scale
    as_bf = lambda a: jnp.asarray(a.astype(np.float32)).astype(BF16)
    return as_bf(cl), as_bf(sl), as_bf(ccb), as_bf(scb)


def _group_ones(width, group):
    g = np.arange(width) // group
    return jnp.asarray((g[:, None] == g[None, :]).astype(np.float32)).astype(BF16)


def _ada_kernel(cond_ref, w_ref, b_ref, o_ref):
    c = cond_ref[...]
    s = (c * jax.nn.sigmoid(c)).astype(BF16)
    o_ref[...] = _dot(s, w_ref[...].astype(BF16)) + b_ref[...]


def _ada_call(cond, w_ada, b_ada):
    tn = 1024
    n = N_MOD * D_MODEL
    return pl.pallas_call(
        _ada_kernel,
        grid=(DEPTH, n // tn),
        in_specs=[
            pl.BlockSpec((MOD_ROWS, D_MODEL), lambda l, j: (0, 0)),
            pl.BlockSpec((None, D_MODEL, tn), lambda l, j: (l, 0, j)),
            pl.BlockSpec((None, 1, tn), lambda l, j: (l, 0, j)),
        ],
        out_specs=pl.BlockSpec((None, MOD_ROWS, tn), lambda l, j: (l, 0, j)),
        out_shape=jax.ShapeDtypeStruct((DEPTH, MOD_ROWS, n), F32),
        compiler_params=_cparams(("parallel", "parallel")),
        name="ada_mod",
    )(cond, w_ada, b_ada.reshape(DEPTH, 1, n))


def _rpb_kernel(rpb_ref, o_ref):
    g = pl.program_id(0)
    n_dr = 2 * NA_ROWS - 1
    n_dc = 2 * NA_COLS - 1
    base = g * (n_dr * n_dc)
    shape = (GRID_W, LANES)
    q = lax.broadcasted_iota(jnp.int32, shape, 0)
    lane = lax.broadcasted_iota(jnp.int32, shape, 1)
    w = lane & (GRID_W - 1)
    dc = jnp.clip(w - q + (NA_COLS - 1), 0, n_dc - 1)
    cs = jnp.clip(q - NA_COLS // 2, 0, GRID_W - NA_COLS)
    col_ok = (w >= cs) & (w < cs + NA_COLS)
    neg = jnp.full(shape, NEG, F32)
    tiles = []
    for d in range(n_dr):
        acc = jnp.zeros(shape, F32)
        for j in range(n_dc):
            acc = jnp.where(dc == j, rpb_ref[base + d * n_dc + j], acc)
        tiles.append(jnp.where(col_ok, acc, neg))
    first = lane < GRID_W
    for e in range(n_dr + 1):
        left = tiles[e - 1] if e >= 1 else neg
        right = tiles[e] if e < n_dr else neg
        o_ref[e] = jnp.where(first, left, right)


def _rpb_call(rpb_d):
    n_e = 2 * NA_ROWS
    return pl.pallas_call(
        _rpb_kernel,
        grid=(DEPTH * N_HEADS,),
        in_specs=[pl.BlockSpec(memory_space=pltpu.SMEM)],
        out_specs=pl.BlockSpec((None, n_e, GRID_W, LANES), lambda g: (g, 0, 0, 0)),
        out_shape=jax.ShapeDtypeStruct((DEPTH * N_HEADS, n_e, GRID_W, LANES), F32),
        compiler_params=_cparams(("parallel",)),
        name="rpb_tiles",
    )(rpb_d.reshape(-1))


def _mod_row(b, n_prompt_blocks, blocks_per_batch):
    return jnp.where(b < n_prompt_blocks, 0, 1 + (b - n_prompt_blocks) // blocks_per_batch)


def _ffn_kernel(*refs, layer, n_prompt_blocks, blocks_per_batch, with_mix, split_in, split_out):
    n_x = 2 if split_in else 1
    x_refs, refs = refs[:n_x], refs[n_x:]
    if with_mix:
        mixp_ref, mixs_ref, wo_ref, g2_ref = refs[:4]
        refs = refs[4:]
    sh_ref, sc_ref, g_ref, gn_ref, wg_hbm, wu_hbm, wd_hbm = refs[:7]
    n_o = 2 if split_out else 1
    o_refs = refs[7:7 + n_o]
    wg_scr, wu_scr, wd_scr, stg_g, stg_u, stg_d, sem, x_scr, a_scr = refs[7 + n_o:]
    b = pl.program_id(0)
    is_prompt = b < n_prompt_blocks
    row = _mod_row(b, n_prompt_blocks, blocks_per_batch)

    def weight_copies(c, slot):
        cols = pl.ds(c * FF_CHUNK, FF_CHUNK)
        return (pltpu.make_async_copy(wg_hbm.at[layer, :, cols], stg_g.at[slot], sem.at[0, slot]),
                pltpu.make_async_copy(wu_hbm.at[layer, :, cols], stg_u.at[slot], sem.at[1, slot]),
                pltpu.make_async_copy(wd_hbm.at[layer, cols, :], stg_d.at[slot], sem.at[2, slot]))

    def block(first):
        if first:
            for slot in range(2):
                for cp in weight_copies(slot, slot):
                    cp.start()
        x = jnp.where(is_prompt, x_refs[0][...], x_refs[1][...]) if split_in else x_refs[0][...]
        if with_mix:
            mix = jnp.where(is_prompt, mixp_ref[...], mixs_ref[...])
            x = x + g2_ref[pl.ds(row, 1), :] * _dot(mix, wo_ref[...])
        x_scr[...] = x
        y = x * lax.rsqrt(jnp.mean(x * x, axis=-1, keepdims=True) + EPS) * gn_ref[...]
        h = (y * (1.0 + sc_ref[pl.ds(row, 1), :]) + sh_ref[pl.ds(row, 1), :]).astype(BF16)
        for c in range(N_FF_CHUNKS):
            lanes = slice(c * FF_CHUNK, (c + 1) * FF_CHUNK)
            if first:
                slot = c % 2
                for cp in weight_copies(c, slot):
                    cp.wait()
                wg_scr[:, lanes] = stg_g[slot].astype(BF16)
                wu_scr[:, lanes] = stg_u[slot].astype(BF16)
                wd_scr[lanes, :] = stg_d[slot].astype(BF16)
                if c + 2 < N_FF_CHUNKS:
                    for cp in weight_copies(c + 2, slot):
                        cp.start()
            gate = _dot(h, wg_scr[:, lanes])
            up = _dot(h, wu_scr[:, lanes])
            a_scr[:, lanes] = ((gate * jax.nn.sigmoid(gate)) * up).astype(BF16)
        out = x_scr[...] + (0.5 * g_ref[pl.ds(row, 1), :]) * _dot(a_scr[...], wd_scr[...])
        if not split_out:
            o_refs[0][...] = out
        elif first:
            o_refs[0][...] = out
        else:
            @pl.when(is_prompt)
            def _():
                o_refs[0][...] = out

            @pl.when(jnp.logical_not(is_prompt))
            def _():
                o_refs[1][...] = out

    pl.when(b == 0)(functools.partial(block, True))
    pl.when(b != 0)(functools.partial(block, False))


def _ffn_call(x, mod, layer, slot, g_norm, w_gate, w_up, w_down, *, n_prompt, mix=None, w_o=None, split_out=False):
    split_in = isinstance(x, tuple)
    t = x[0].shape[0] + x[1].shape[0] if split_in else x.shape[0]
    tm = TOK_BLOCK
    npb = n_prompt // tm
    with_mix = mix is not None
    mod_spec = lambda k: pl.BlockSpec((None, MOD_ROWS, D_MODEL), lambda b: (layer, 0, k))
    x_spec = pl.BlockSpec((tm, D_MODEL), lambda b: (b, 0))
    p_spec = pl.BlockSpec((tm, D_MODEL), lambda b: (jnp.minimum(b, npb - 1), 0))
    s_spec = pl.BlockSpec((tm, D_MODEL), lambda b: (jnp.maximum(b - npb, 0), 0))
    hbm = pl.BlockSpec(memory_space=pl.ANY)
    in_specs = [p_spec, s_spec] if split_in else [x_spec]
    args = list(x) if split_in else [x]
    if with_mix:
        in_specs += [p_spec, s_spec, pl.BlockSpec((None, D_MODEL, D_MODEL), lambda b: (layer, 0, 0)), mod_spec(5)]
        args += [mix[0], mix[1], w_o, mod]
    c0 = 3 * slot
    in_specs += [mod_spec(c0), mod_spec(c0 + 1), mod_spec(c0 + 2),
                 pl.BlockSpec((None, 1, D_MODEL), lambda b: (layer, 0, 0)), hbm, hbm, hbm]
    args += [mod, mod, mod, g_norm.reshape(DEPTH, 1, D_MODEL), w_gate, w_up, w_down]
    scratch = [pltpu.VMEM((D_MODEL, D_FF), BF16), pltpu.VMEM((D_MODEL, D_FF), BF16), pltpu.VMEM((D_FF, D_MODEL), BF16),
               pltpu.VMEM((2, D_MODEL, FF_CHUNK), F32), pltpu.VMEM((2, D_MODEL, FF_CHUNK), F32),
               pltpu.VMEM((2, FF_CHUNK, D_MODEL), F32), pltpu.SemaphoreType.DMA((3, 2)),
               pltpu.VMEM((tm, D_MODEL), F32), pltpu.VMEM((tm, D_FF), BF16)]
    if split_out:
        out_specs = [p_spec, s_spec]
        out_shape = [jax.ShapeDtypeStruct((n_prompt, D_MODEL), F32), jax.ShapeDtypeStruct((t - n_prompt, D_MODEL), F32)]
    else:
        out_specs = x_spec
        out_shape = jax.ShapeDtypeStruct((t, D_MODEL), F32)
    return pl.pallas_call(
        functools.partial(_ffn_kernel, layer=layer, n_prompt_blocks=npb, blocks_per_batch=DEC_SEQ // tm,
                          with_mix=with_mix, split_in=split_in, split_out=split_out),
        grid=(t // tm,),
        in_specs=in_specs,
        out_specs=out_specs,
        out_shape=out_shape,
        scratch_shapes=scratch,
        compiler_params=_cparams(("arbitrary",)),
        name="ffn_mix" if with_mix else "ffn",
    )(*args)


def _swap(x, width):
    lane = lax.broadcasted_iota(jnp.int32, x.shape, 1)
    first = (lane & width) == 0
    return jnp.where(first, pltpu.roll(x, LANES - width, 1), pltpu.roll(x, width, 1))


def _rope(x, cos, sin, width):
    parts = []
    for c in range(x.shape[1] // LANES):
        xc = x[:, c * LANES:(c + 1) * LANES]
        parts.append(xc * cos + _swap(xc, width) * sin)
    return parts[0] if len(parts) == 1 else jnp.concatenate(parts, axis=1)


def _head_norm(x, ones, gain, inv_dim):
    ms = _group_sums(x * x, ones) * inv_dim
    return x * lax.rsqrt(ms + EPS) * gain


def _mla_keys(ckv_n_bf16, kr_pad, w_ukv, ones128, kn_a):
    kv = _dot(ckv_n_bf16, w_ukv)
    k = kv[:, :512] + jnp.concatenate([kr_pad] * N_HEADS, axis=1)
    return _head_norm(k, ones128, kn_a, 1.0 / A_QK), kv[:, 512:]


def _prep_kernel(*refs, n_prompt_blocks, blocks_per_batch, n_alias):
    (x_ref, sh_ref, sc_ref, gn_ref, win_ref, gqa_ref, wuq_ref, gkva_ref, wukv_ref,
     qna_ref, kna_ref, qnb_ref, knb_ref, qnd_ref, knd_ref,
     ones128_ref, ones64_ref, cos_a_ref, sin_a_ref, cos_b_ref, sin_b_ref) = refs[:21]
    feat_ref, ckv_ref, kr_ref, wk_ref, wv_ref, nk_ref, nv_ref, win_scr = refs[21 + n_alias:]
    b = pl.program_id(0)
    row = _mod_row(b, n_prompt_blocks, blocks_per_batch)
    is_prompt = b < n_prompt_blocks

    @pl.when(b == 0)
    def _():
        win_scr[0:P_KR + A_ROPE, :] = win_ref[0:W_IN_KR + A_ROPE, :].astype(BF16)
        win_scr[P_KR + A_ROPE:P_QB, :] = jnp.zeros((LANES - A_ROPE, D_MODEL), BF16)
        for p, hd in enumerate(B_ORDER):
            win_scr[P_QB + HEAD_DIM * p:P_QB + HEAD_DIM * (p + 1), :] = (
                win_ref[W_IN_QB + HEAD_DIM * hd:W_IN_QB + HEAD_DIM * (hd + 1), :].astype(BF16))
        for r in range(W_IN_KB, W_IN_WIDTH, PREP_BLOCK):
            win_scr[r + P_KB - W_IN_KB:r + P_KB - W_IN_KB + PREP_BLOCK, :] = win_ref[r:r + PREP_BLOCK, :].astype(BF16)

    x = x_ref[...]
    y = x * lax.rsqrt(jnp.mean(x * x, axis=-1, keepdims=True) + EPS) * gn_ref[...]
    h = (y * (1.0 + sc_ref[pl.ds(row, 1), :]) + sh_ref[pl.ds(row, 1), :]).astype(BF16)
    proj = _dot_t(h, win_scr[...])
    ones128 = ones128_ref[...]
    ones64 = ones64_ref[...]

    cq = proj[:, P_CQ:P_CQ + 256]
    cqn = cq * lax.rsqrt(jnp.mean(cq * cq, axis=-1, keepdims=True) + EPS) * gqa_ref[...]
    qa = _head_norm(_dot(cqn.astype(BF16), wuq_ref[...]), ones128, qna_ref[...], 1.0 / A_QK)
    ckv = proj[:, P_CKV:P_CKV + 128]
    ckv_n = ckv * lax.rsqrt(jnp.mean(ckv * ckv, axis=-1, keepdims=True) + EPS) * gkva_ref[...]
    kr_pad = proj[:, P_KR:P_KR + 128]
    ka, va = _mla_keys(ckv_n.astype(BF16), kr_pad, wukv_ref[...], ones128, kna_ref[...])
    cos_a, sin_a = cos_a_ref[...], sin_a_ref[...]
    qa = _rope(qa, cos_a, sin_a, 8)
    ka = _rope(ka, cos_a, sin_a, 8)
    feat_ref[:, QA:QA + 512] = (qa * (A_QK ** -0.5)).astype(BF16)
    feat_ref[:, KA:KA + 512] = ka.astype(BF16)
    feat_ref[:, VA:VA + 256] = va.astype(BF16)

    qb = _head_norm(proj[:, P_QB:P_QB + 256], ones64, qnb_ref[...], 1.0 / HEAD_DIM)
    kb = _head_norm(proj[:, P_KB:P_KB + 128], ones64[:128, :128], knb_ref[:, :128], 1.0 / HEAD_DIM)
    vb = proj[:, P_VB:P_VB + 128]
    cos_b, sin_b = cos_b_ref[...], sin_b_ref[...]
    qb = _rope(qb, cos_b, sin_b, 16)
    kb = _rope(kb, cos_b, sin_b, 16)
    feat_ref[:, QB:QB + 256] = (qb * (HEAD_DIM ** -0.5)).astype(BF16)
    feat_ref[:, KB:KB + 128] = kb.astype(BF16)
    feat_ref[:, VB:VB + 128] = vb.astype(BF16)

    feat_ref[:, XC:XC + 256] = proj[:, P_XC:P_XC + 256].astype(BF16)

    qd = _head_norm(proj[:, P_QD:P_QD + 256], ones64, qnd_ref[...], 1.0 / HEAD_DIM)
    kd = _head_norm(proj[:, P_KD:P_KD + 256], ones64, knd_ref[...], 1.0 / HEAD_DIM)
    vd = proj[:, P_VD:P_VD + 256]
    feat_ref[:, QD:QD + 256] = (qd * (HEAD_DIM ** -0.5)).astype(BF16)
    feat_ref[:, KD:KD + 256] = kd.astype(BF16)
    feat_ref[:, VD:VD + 256] = vd.astype(BF16)

    @pl.when(is_prompt)
    def _():
        ckv_ref[...] = ckv_n
        kr_ref[...] = kr_pad[:, :A_ROPE]
        wk_ref[...] = kb
        wv_ref[...] = vb
        nk_ref[...] = kd
        nv_ref[...] = vd


_PREP_WEIGHTS = ("g_mix", "w_in", "g_qa", "w_uq", "g_kva", "w_ukv",
                 "qn_a", "kn_a", "qn_b", "kn_b", "qn_d", "kn_d")
_CACHE_WIDTHS = (128, A_ROPE, 128, 128, 256, 256)


def _prep_call(x, mod, layer, lw, consts, new_caches, *, n_prompt):
    t = x.shape[0]
    tm = PREP_BLOCK
    npb = n_prompt // tm
    bpb = DEC_SEQ // tm
    batch = n_prompt // SEQ
    full = lambda a: pl.BlockSpec(a.shape, lambda b: (0,) * a.ndim)
    mod_spec = lambda k: pl.BlockSpec((None, MOD_ROWS, D_MODEL), lambda b: (layer, 0, k))
    lyr = lambda a: pl.BlockSpec((None,) + a.shape[1:], lambda b: (layer,) + (0,) * (a.ndim - 1))
    tok = lambda w: pl.BlockSpec((tm, w), lambda b: (b, 0))
    pos = pl.BlockSpec((tm, LANES), lambda b: (jnp.where(b < npb, 0, 1 + (b - npb) % bpb), 0))
    weights = [lw[k] for k in _PREP_WEIGHTS]
    tabs = [consts["cos_a"], consts["sin_a"], consts["cos_b"], consts["sin_b"]]
    in_specs = [tok(D_MODEL), mod_spec(3), mod_spec(4)] + [lyr(a) for a in weights]
    in_specs += [full(consts["ones128"]), full(consts["ones64"])] + [pos] * len(tabs)
    args = [x, mod, mod] + weights + [consts["ones128"], consts["ones64"]] + tabs
    aliases = {len(args) + k: 1 + k for k in range(len(new_caches))}
    in_specs += [pl.BlockSpec(memory_space=pl.ANY)] * len(new_caches)
    args += list(new_caches)
    spb = SEQ // tm
    cache_spec = lambda w: pl.BlockSpec(
        (None, None, tm, w), lambda b: (jnp.minimum(b, npb - 1) // spb, layer, jnp.minimum(b, npb - 1) % spb, 0))
    out_specs = [tok(FEAT)] + [cache_spec(w) for w in _CACHE_WIDTHS]
    out_shape = [jax.ShapeDtypeStruct((t, FEAT), BF16)]
    out_shape += [jax.ShapeDtypeStruct((batch, DEPTH, SEQ, w), F32) for w in _CACHE_WIDTHS]
    feat, *caches = pl.pallas_call(
        functools.partial(_prep_kernel, n_prompt_blocks=npb, blocks_per_batch=bpb, n_alias=len(aliases)),
        grid=(t // tm,),
        in_specs=in_specs,
        out_specs=out_specs,
        out_shape=out_shape,
        input_output_aliases=aliases,
        scratch_shapes=[pltpu.VMEM((P_WIDTH, D_MODEL), BF16)],
        compiler_params=_cparams(("arbitrary",)),
        name="prep",
    )(*args)
    return feat, caches


def _slot_mask(shape, width, slot):
    lane = lax.broadcasted_iota(jnp.int32, shape, 1)
    return (lane >= slot * width) & (lane < (slot + 1) * width)


def _softmax_pv(scores, values, slot, sink=None):
    m = functools.reduce(jnp.maximum, [jnp.max(s, axis=-1, keepdims=True) for s in scores])
    if sink is not None:
        m = jnp.maximum(m, sink)
    one = jnp.ones((), BF16)
    o = functools.reduce(jnp.add, [
        _dot(jnp.exp(s - m).astype(BF16), jnp.where(_slot_mask(v.shape, HEAD_DIM, slot), v, one))
        for s, v in zip(scores, values)])
    groups = [pltpu.roll(o[:, c * LANES:(c + 1) * LANES], HEAD_DIM, 1) for c in range(o.shape[1] // LANES)]
    den = groups[0] if len(groups) == 1 else jnp.concatenate(groups, axis=1)
    if sink is not None:
        den = den + jnp.exp(sink - m)
    return o / den


def _attn_ctx_kernel(sink_ref, feat_ref, cl_ref, sl_ref, ccb_ref, scb_ref, o_ref):
    n = feat_ref.shape[0]
    zero = jnp.zeros((), BF16)

    va = feat_ref[:, VA:VA + 256]
    oa = jnp.zeros((n, 256), F32)
    for h in range(N_HEADS):
        q = feat_ref[:, QA + 128 * h:QA + 128 * (h + 1)]
        k = feat_ref[:, KA + 128 * h:KA + 128 * (h + 1)]
        o = _softmax_pv([_dot_t(q, k)], [va], h)
        oa = jnp.where(_slot_mask(o.shape, HEAD_DIM, h), o, oa)
    o_ref[:, 0:256] = oa.astype(BF16)

    kb = feat_ref[:, KB:KB + 128]
    vb = feat_ref[:, VB:VB + 128]
    for c in range(2):
        qc = feat_ref[:, QB + 128 * c:QB + 128 * (c + 1)]
        ob = jnp.zeros((n, 128), F32)
        for half in range(2):
            head = B_ORDER[2 * c + half]
            msk = _slot_mask(qc.shape, HEAD_DIM, half)
            o = _softmax_pv([_dot_t(jnp.where(msk, qc, zero), kb)], [vb], half, sink=sink_ref[head])
            ob = jnp.where(msk, o, ob)
        o_ref[:, 256 + 128 * c:256 + 128 * (c + 1)] = ob.astype(BF16)

    xc = feat_ref[:, XC:XC + 256]
    fa = _dot(xc, ccb_ref[...]).astype(BF16)
    fb = _dot(xc, scb_ref[...]).astype(BF16)
    o_ref[:, 512:768] = (_dot(cl_ref[...], fa) - _dot(sl_ref[...], fb)).astype(BF16)

    qd = feat_ref[:, QD:QD + 256]
    kd = feat_ref[:, KD:KD + 256]
    vd = feat_ref[:, VD:VD + 256]
    od = jnp.zeros((n, 256), F32)
    for h in range(N_HEADS):
        msk = _slot_mask(qd.shape, HEAD_DIM, h)
        o = _softmax_pv([_dot_t(jnp.where(msk, qd, zero), kd)], [vd], h)
        od = jnp.where(msk, o, od)
    o_ref[:, 768:1024] = od.astype(BF16)


def _attn_ctx_call(feat, sink, consts, *, n_prompt):
    t = n_prompt
    full = lambda a: pl.BlockSpec(a.shape, lambda i: (0,) * a.ndim)
    tabs = [consts["cl_ctx"], consts["sl_ctx"], consts["ccb_ctx"], consts["scb_ctx"]]
    return pl.pallas_call(
        _attn_ctx_kernel,
        grid=(t // SEQ,),
        in_specs=[pl.BlockSpec(memory_space=pltpu.SMEM),
                  pl.BlockSpec((SEQ, FEAT), lambda i: (i, 0))] + [full(a) for a in tabs],
        out_specs=pl.BlockSpec((SEQ, D_MODEL), lambda i: (i, 0)),
        out_shape=jax.ShapeDtypeStruct((t, D_MODEL), BF16),
        compiler_params=_cparams(("parallel",)),
        name="attn_ctx",
    )(sink, feat, *tabs)


def _attn_lat_kernel(sink_ref, feat_ref, cckv_ref, ckr_ref, cwk_ref, cwv_ref, cnk_ref, cnv_ref,
                     wukv_ref, kna_ref, ones128_ref, tt_ref, cl_ref, sl_ref, ccb_ref, scb_ref,
                     o_ref, kac_scr, vac_scr, fa_scr, fb_scr):
    j = pl.program_id(1)
    nq = Q_BLOCK
    zero = jnp.zeros((), BF16)

    @pl.when(j == 0)
    def _():
        ka_c, va_c = _mla_keys(cckv_ref[...].astype(BF16), ckr_ref[...], wukv_ref[...],
                               ones128_ref[...], kna_ref[...])
        kac_scr[...] = ka_c.astype(BF16)
        vac_scr[...] = va_c.astype(BF16)
        xc = feat_ref[:, XC:XC + 256]
        fa_scr[...] = _dot(xc, ccb_ref[...]).astype(BF16)
        fb_scr[...] = _dot(xc, scb_ref[...]).astype(BF16)

    r0 = pl.multiple_of(j * nq, nq)
    rows = pl.ds(r0, nq)

    va = feat_ref[:, VA:VA + 256]
    va_c = vac_scr[...]
    oa = jnp.zeros((nq, 256), F32)
    for h in range(N_HEADS):
        q = feat_ref[rows, QA + 128 * h:QA + 128 * (h + 1)]
        s_c = _dot_t(q, kac_scr[:, 128 * h:128 * (h + 1)])
        s_l = _dot_t(q, feat_ref[:, KA + 128 * h:KA + 128 * (h + 1)])
        o = _softmax_pv([s_c, s_l], [va_c, va], h)
        oa = jnp.where(_slot_mask(o.shape, HEAD_DIM, h), o, oa)
    o_ref[:, 0:256] = oa.astype(BF16)

    start = pl.multiple_of(jnp.clip(r0 - B_WINDOW, 0, DEC_SEQ - B_KEYS), B_WINDOW)
    kw = feat_ref[pl.ds(start, B_KEYS), KB:KB + 128]
    vw = feat_ref[pl.ds(start, B_KEYS), VB:VB + 128]
    kb_c = cwk_ref[...].astype(BF16)
    vb_c = cwv_ref[...].astype(BF16)
    qpos = r0 + lax.broadcasted_iota(jnp.int32, (nq, B_KEYS), 0)
    kpos = start + lax.broadcasted_iota(jnp.int32, (nq, B_KEYS), 1)
    band = jnp.abs(qpos - kpos) <= B_WINDOW
    for c in range(2):
        qc = feat_ref[rows, QB + 128 * c:QB + 128 * (c + 1)]
        ob = jnp.zeros((nq, 128), F32)
        for half in range(2):
            head = B_ORDER[2 * c + half]
            msk = _slot_mask(qc.shape, HEAD_DIM, half)
            q = jnp.where(msk, qc, zero)
            s_l = jnp.where(band, _dot_t(q, kw), NEG)
            s_c = _dot_t(q, kb_c)
            o = _softmax_pv([s_l, s_c], [vw, vb_c], half, sink=sink_ref[head])
            ob = jnp.where(msk, o, ob)
        o_ref[:, 256 + 128 * c:256 + 128 * (c + 1)] = ob.astype(BF16)

    o_ref[:, 512:768] = (_dot(cl_ref[rows, :], fa_scr[...]) - _dot(sl_ref[rows, :], fb_scr[...])).astype(BF16)

    rows_per_block = nq // GRID_W
    n_rows = DEC_SEQ // GRID_W
    kr_win = min(NA_ROWS, n_rows)
    ks = jnp.where(j < 2, 0, n_rows - D_KEY_ROWS)
    k0_tok = pl.multiple_of(ks * GRID_W, GRID_W)
    keys = pl.ds(k0_tok, D_KEY_ROWS * GRID_W)
    kdw = feat_ref[keys, KD:KD + 256]
    vdw = feat_ref[keys, VD:VD + 256]
    kd_c = cnk_ref[...].astype(BF16)
    vd_c = cnv_ref[...].astype(BF16)
    qd = feat_ref[rows, QD:QD + 256]
    first = lax.broadcasted_iota(jnp.int32, (GRID_W, LANES), 1) < GRID_W
    od = jnp.zeros((nq, 256), F32)
    for h in range(N_HEADS):
        bias_rows = []
        for rr in range(rows_per_block):
            r = j * rows_per_block + rr
            rs = jnp.clip(r - kr_win // 2, 0, n_rows - kr_win)
            tiles = []
            for p in range(D_KEY_ROWS // 2):
                ka0 = ks + 2 * p
                e = jnp.clip(ka0 - r + NA_ROWS, 0, 2 * NA_ROWS - 1)
                ok0 = ((ka0 >= rs) & (ka0 < rs + kr_win)).astype(jnp.int32)
                ok1 = ((ka0 + 1 >= rs) & (ka0 + 1 < rs + kr_win)).astype(jnp.int32)
                ok = jnp.where(first, ok0, ok1) > 0
                tiles.append(jnp.where(ok, tt_ref[h, e], NEG))
            bias_rows.append(jnp.concatenate(tiles, axis=1))
        bias = jnp.concatenate(bias_rows, axis=0)
        msk = _slot_mask(qd.shape, HEAD_DIM, h)
        q = jnp.where(msk, qd, zero)
        s_l = _dot_t(q, kdw) + bias
        s_c = _dot_t(q, kd_c)
        o = _softmax_pv([s_l, s_c], [vdw, vd_c], h)
        od = jnp.where(msk, o, od)
    o_ref[:, 768:1024] = od.astype(BF16)


def _attn_lat_call(feat, sink, layer, caches, lw, consts, tt, *, n_prompt):
    first = n_prompt // DEC_SEQ
    nb = feat.shape[0] // DEC_SEQ - first
    full = lambda a: pl.BlockSpec(a.shape, lambda b, j: (0,) * a.ndim)
    lyr = lambda a: pl.BlockSpec((None,) + a.shape[1:], lambda b, j: (layer,) + (0,) * (a.ndim - 1))
    cache = lambda a: pl.BlockSpec((None, None) + a.shape[2:], lambda b, j: (b, layer, 0, 0))
    tabs = [consts["cl_lat"], consts["sl_lat"], consts["ccb_lat"], consts["scb_lat"]]
    in_specs = [pl.BlockSpec(memory_space=pltpu.SMEM),
                pl.BlockSpec((DEC_SEQ, FEAT), lambda b, j: (first + b, 0))]
    in_specs += [cache(a) for a in caches]
    in_specs += [lyr(lw["w_ukv"]), lyr(lw["kn_a"]), full(consts["ones128"]),
                 pl.BlockSpec((N_HEADS,) + tt.shape[1:], lambda b, j: (layer, 0, 0, 0))]
    in_specs += [full(a) for a in tabs]
    return pl.pallas_call(
        _attn_lat_kernel,
        grid=(nb, DEC_SEQ // Q_BLOCK),
        in_specs=in_specs,
        out_specs=pl.BlockSpec((Q_BLOCK, D_MODEL), lambda b, j: (b * (DEC_SEQ // Q_BLOCK) + j, 0)),
        out_shape=jax.ShapeDtypeStruct((nb * DEC_SEQ, D_MODEL), BF16),
        scratch_shapes=[pltpu.VMEM((PAST_LEN, 512), BF16), pltpu.VMEM((PAST_LEN, 256), BF16),
                        pltpu.VMEM((DEC_SEQ, 256), BF16), pltpu.VMEM((DEC_SEQ, 256), BF16)],
        compiler_params=_cparams(("parallel", "arbitrary")),
        name="attn_lat",
    )(sink, feat, *caches, lw["w_ukv"], lw["kn_a"], consts["ones128"], tt, *tabs)


def _layer_weights(w_in, w_uq, w_ukv, w_o, g_mix, g_qa, g_kva, qn_a, kn_a, qn_b, kn_b, qn_d, kn_d):
    z = lambda *s: jnp.zeros(s, F32)

    def slots(a, rows):
        parts = []
        for h in range(N_HEADS):
            hd = a[..., A_QK * h:A_QK * (h + 1)]
            parts += [hd[..., HEAD_DIM:], z(*rows, LANES - A_QK), hd[..., :HEAD_DIM]]
        return jnp.concatenate(parts, axis=-1)

    w_uq_p = slots(w_uq, (DEPTH, 256)).astype(BF16)
    k_cols, v_cols = [], []
    for h in range(N_HEADS):
        k_cols += [z(DEPTH, 128, HEAD_DIM), w_ukv[..., 128 * h:128 * h + HEAD_DIM]]
        v_cols.append(w_ukv[..., 128 * h + HEAD_DIM:128 * (h + 1)])
    w_ukv_p = jnp.concatenate(k_cols + v_cols, axis=-1).astype(BF16)
    w_o_p = jnp.concatenate(
        [w_o[:, :256]] + [w_o[:, 256 + HEAD_DIM * h:256 + HEAD_DIM * (h + 1)] for h in B_ORDER]
        + [w_o[:, 512:]], axis=1).astype(BF16)
    gain_a = lambda g: slots(jnp.tile(g, (1, N_HEADS)), (DEPTH,))[:, None, :]
    tile4 = lambda g: jnp.tile(g, (1, N_HEADS))[:, None, :]
    lw = dict(w_in=jnp.swapaxes(w_in, 1, 2), w_uq=w_uq_p, w_ukv=w_ukv_p,
              g_mix=g_mix[:, None, :], g_qa=g_qa[:, None, :], g_kva=g_kva[:, None, :],
              qn_a=gain_a(qn_a), kn_a=gain_a(kn_a), qn_b=tile4(qn_b), kn_b=tile4(kn_b),
              qn_d=tile4(qn_d), kn_d=tile4(kn_d))
    return lw, w_o_p


def _constants():
    consts = {}
    ident = lambda cs: (jnp.concatenate([jnp.ones((PREP_BLOCK, LANES), F32), cs[0]]),
                        jnp.concatenate([jnp.zeros((PREP_BLOCK, LANES), F32), cs[1]]))
    consts["cos_a"], consts["sin_a"] = ident(_rope_tables(A_ROPE, LANES, DEC_SEQ))
    consts["cos_b"], consts["sin_b"] = ident(_rope_tables(HEAD_DIM, HEAD_DIM, DEC_SEQ))
    for tag, n in (("ctx", SEQ), ("lat", DEC_SEQ)):
        cl, sl, ccb, scb = _fourier_constants(n)
        consts["cl_" + tag], consts["sl_" + tag], consts["ccb_" + tag], consts["scb_" + tag] = cl, sl, ccb, scb
    consts["ones128"] = _group_ones(512, LANES)
    consts["ones64"] = _group_ones(256, HEAD_DIM)
    return consts


def kernel(x_prompt, x_sample, cache_mla_ckv, cache_mla_krope, cache_win_k, cache_win_v, cache_na_k, cache_na_v, c, c_ctx, w_ada, b_ada, g_ffn1, w_gate1, w_up1, w_down1, g_mix, w_in, g_qa, w_uq, g_kva, w_ukv, qn_a, kn_a, qn_b, kn_b, sink_b, qn_d, kn_d, rpb_d, w_o, g_ffn2, w_gate2, w_up2, w_down2):
    batch, seq, _ = x_prompt.shape
    dec_batch, dec_seq, _ = x_sample.shape
    assert (seq, dec_seq, dec_batch + 1 <= MOD_ROWS) == (SEQ, DEC_SEQ, True)

    consts = _constants()
    lw, w_o_p = _layer_weights(w_in, w_uq, w_ukv, w_o, g_mix, g_qa, g_kva, qn_a, kn_a, qn_b, kn_b, qn_d, kn_d)

    cond = jnp.concatenate([c_ctx[None, :], c, jnp.zeros((MOD_ROWS - 1 - dec_batch, D_MODEL), F32)], axis=0)
    mod = _ada_call(cond, w_ada, b_ada)
    tt = _rpb_call(rpb_d)

    caches = [cache_mla_ckv,
              jnp.pad(cache_mla_krope, ((0, 0), (0, 0), (0, 0), (0, LANES - A_ROPE))),
              cache_win_k.reshape(dec_batch, DEPTH, PAST_LEN, 128),
              cache_win_v.reshape(dec_batch, DEPTH, PAST_LEN, 128),
              cache_na_k.reshape(dec_batch, DEPTH, PAST_LEN, 256),
              cache_na_v.reshape(dec_batch, DEPTH, PAST_LEN, 256)]

    n_prompt = batch * seq
    x = (x_prompt.reshape(n_prompt, D_MODEL), x_sample.reshape(dec_batch * dec_seq, D_MODEL))
    new = [jnp.zeros((batch, DEPTH, seq, w), F32) for w in _CACHE_WIDTHS]
    for l in range(DEPTH):
        sink = sink_b[l]
        x = _ffn_call(x, mod, l, 0, g_ffn1, w_gate1, w_up1, w_down1, n_prompt=n_prompt)
        feat, new = _prep_call(x, mod, l, lw, consts, new, n_prompt=n_prompt)
        mix = (_attn_ctx_call(feat, sink, consts, n_prompt=n_prompt),
               _attn_lat_call(feat, sink, l, caches, lw, consts, tt, n_prompt=n_prompt))
        x = _ffn_call(x, mod, l, 2, g_ffn2, w_gate2, w_up2, w_down2, n_prompt=n_prompt, mix=mix, w_o=w_o_p,
                      split_out=l == DEPTH - 1)

    heads = lambda a, n: a.reshape(batch, DEPTH, seq, n, HEAD_DIM)
    return (x[0].reshape(batch, seq, D_MODEL), x[1].reshape(dec_batch, dec_seq, D_MODEL),
            new[0], new[1], heads(new[2], 2), heads(new[3], 2), heads(new[4], N_HEADS), heads(new[5], N_HEADS))
```

```python
import functools

import numpy as np
import jax
import jax.numpy as jnp
from jax import lax
from jax.experimental import pallas as pl
from jax.experimental.pallas import tpu as pltpu

F32 = jnp.float32
BF16 = jnp.bfloat16

D_MODEL = 1024
DEPTH = 4
SEQ = 256
DEC_SEQ = 1024
PAST_LEN = 256
GRID_W = 64
D_FF = 2816
N_MOD = 9
EPS = 1e-6
ROPE_BASE = 10000.0
A_QK = 96
A_ROPE = 32
B_WINDOW = 128
NA_ROWS = 8
NA_COLS = 16
HEAD_DIM = 64
N_HEADS = 4
NEG = -1e30

LANES = 128
MOD_ROWS = 8
VMEM_LIMIT = 56 * 1024 * 1024

QA, KA, VA = 0, 512, 1024
QB, KB, VB = 1280, 1536, 1664
XC = 1792
QD, KD, VD = 2048, 2304, 2560
FEAT = 2816
B_ORDER = (0, 2, 1, 3)

W_IN_KR, W_IN_QB, W_IN_KB, W_IN_WIDTH = 384, 416, 672, 1952
P_CQ, P_CKV, P_KR, P_QB, P_KB, P_VB, P_XC, P_QD, P_KD, P_VD = (
    0, 256, 384, 512, 768, 896, 1024, 1280, 1536, 1792)
P_WIDTH = 2048

TOK_BLOCK = 512
PREP_BLOCK = 256
FF_CHUNK = 256
N_FF_CHUNKS = D_FF // FF_CHUNK
Q_BLOCK = 256
B_KEYS = 512
D_KEY_ROWS = 12


def _cparams(sem):
    return pltpu.CompilerParams(dimension_semantics=sem, vmem_limit_bytes=VMEM_LIMIT)


def _dot(a, b):
    return jnp.dot(a, b, preferred_element_type=F32)


def _dot_t(a, b):
    return lax.dot_general(a, b, (((1,), (1,)), ((), ())), preferred_element_type=F32)


def _group_sums(sq, ones_bf16):
    return _dot(sq.astype(BF16), ones_bf16)


def _rope_tables(width, slot, n_pos):
    half = width // 2
    quarter = half // 2
    inv = (np.float32(ROPE_BASE) ** (-np.arange(0, half, 2, dtype=np.float32) / np.float32(half))).astype(np.float32)
    t = np.arange(n_pos)
    pos = np.stack([t // GRID_W, t % GRID_W], axis=0).astype(np.float32)
    cos = np.ones((n_pos, LANES), np.float32)
    sin = np.zeros((n_pos, LANES), np.float32)
    for lane in range(LANES):
        i = lane % slot
        if i >= width:
            continue
        axis = i // half
        j = i % half
        ang = pos[axis] * inv[j % quarter]
        cos[:, lane] = np.cos(ang)
        sin[:, lane] = np.sin(ang) * (-1.0 if j < quarter else 1.0)
    return jnp.asarray(cos), jnp.asarray(sin)


def _dft_tables(n):
    k = np.arange(n)
    ang = 2.0 * np.pi * ((k[:, None] * k[None, :]) % n) / n
    return np.cos(ang), np.sin(ang)


def _fourier_constants(n_pos):
    cl, sl = _dft_tables(n_pos)
    cc, sc = _dft_tables(HEAD_DIM)
    scale = 1.0 / np.sqrt(float(HEAD_DIM * n_pos))
    eye = np.eye(N_HEADS)
    ccb = np.kron(eye, cc) * scale
    scb = np.kron(eye, sc) * scale
    as_bf = lambda a: jnp.asarray(a.astype(np.float32)).astype(BF16)
    return as_bf(cl), as_bf(sl), as_bf(ccb), as_bf(scb)


def _group_ones(width, group):
    g = np.arange(width) // group
    return jnp.asarray((g[:, None] == g[None, :]).astype(np.float32)).astype(BF16)


def _ada_kernel(cond_ref, w_ref, b_ref, o_ref):
    c = cond_ref[...]
    s = (c * jax.nn.sigmoid(c)).astype(BF16)
    o_ref[...] = _dot(s, w_ref[...].astype(BF16)) + b_ref[...]


def _ada_call(cond, w_ada, b_ada):
    tn = 1024
    n = N_MOD * D_MODEL
    return pl.pallas_call(
        _ada_kernel,
        grid=(DEPTH, n // tn),
        in_specs=[
            pl.BlockSpec((MOD_ROWS, D_MODEL), lambda l, j: (0, 0)),
            pl.BlockSpec((None, D_MODEL, tn), lambda l, j: (l, 0, j)),
            pl.BlockSpec((None, 1, tn), lambda l, j: (l, 0, j)),
        ],
        out_specs=pl.BlockSpec((None, MOD_ROWS, tn), lambda l, j: (l, 0, j)),
        out_shape=jax.ShapeDtypeStruct((DEPTH, MOD_ROWS, n), F32),
        compiler_params=_cparams(("parallel", "parallel")),
        name="ada_mod",
    )(cond, w_ada, b_ada.reshape(DEPTH, 1, n))


def _rpb_kernel(rpb_ref, o_ref):
    g = pl.program_id(0)
    n_dr = 2 * NA_ROWS - 1
    n_dc = 2 * NA_COLS - 1
    base = g * (n_dr * n_dc)
    shape = (GRID_W, LANES)
    q = lax.broadcasted_iota(jnp.int32, shape, 0)
    lane = lax.broadcasted_iota(jnp.int32, shape, 1)
    w = lane & (GRID_W - 1)
    dc = jnp.clip(w - q + (NA_COLS - 1), 0, n_dc - 1)
    cs = jnp.clip(q - NA_COLS // 2, 0, GRID_W - NA_COLS)
    col_ok = (w >= cs) & (w < cs + NA_COLS)
    neg = jnp.full(shape, NEG, F32)
    tiles = []
    for d in range(n_dr):
        acc = jnp.zeros(shape, F32)
        for j in range(n_dc):
            acc = jnp.where(dc == j, rpb_ref[base + d * n_dc + j], acc)
        tiles.append(jnp.where(col_ok, acc, neg))
    first = lane < GRID_W
    for e in range(n_dr + 1):
        left = tiles[e - 1] if e >= 1 else neg
        right = tiles[e] if e < n_dr else neg
        o_ref[e] = jnp.where(first, left, right)


def _rpb_call(rpb_d):
    n_e = 2 * NA_ROWS
    return pl.pallas_call(
        _rpb_kernel,
        grid=(DEPTH * N_HEADS,),
        in_specs=[pl.BlockSpec(memory_space=pltpu.SMEM)],
        out_specs=pl.BlockSpec((None, n_e, GRID_W, LANES), lambda g: (g, 0, 0, 0)),
        out_shape=jax.ShapeDtypeStruct((DEPTH * N_HEADS, n_e, GRID_W, LANES), F32),
        compiler_params=_cparams(("parallel",)),
        name="rpb_tiles",
    )(rpb_d.reshape(-1))


def _mod_row(b, n_prompt_blocks, blocks_per_batch):
    return jnp.where(b < n_prompt_blocks, 0, 1 + (b - n_prompt_blocks) // blocks_per_batch)


def _ffn_kernel(*refs, layer, n_prompt_blocks, blocks_per_batch, with_mix, split_in, split_out):
    n_x = 2 if split_in else 1
    x_refs, refs = refs[:n_x], refs[n_x:]
    if with_mix:
        mixp_ref, mixs_ref, wo_ref, g2_ref = refs[:4]
        refs = refs[4:]
    sh_ref, sc_ref, g_ref, gn_ref, wg_hbm, wu_hbm, wd_hbm = refs[:7]
    n_o = 2 if split_out else 1
    o_refs = refs[7:7 + n_o]
    wg_scr, wu_scr, wd_scr, stg_g, stg_u, stg_d, sem, x_scr, a_scr = refs[7 + n_o:]
    b = pl.program_id(0)
    is_prompt = b < n_prompt_blocks
    row = _mod_row(b, n_prompt_blocks, blocks_per_batch)

    def weight_copies(c, slot):
        cols = pl.ds(c * FF_CHUNK, FF_CHUNK)
        return (pltpu.make_async_copy(wg_hbm.at[layer, :, cols], stg_g.at[slot], sem.at[0, slot]),
                pltpu.make_async_copy(wu_hbm.at[layer, :, cols], stg_u.at[slot], sem.at[1, slot]),
                pltpu.make_async_copy(wd_hbm.at[layer, cols, :], stg_d.at[slot], sem.at[2, slot]))

    def block(first):
        if first:
            for slot in range(2):
                for cp in weight_copies(slot, slot):
                    cp.start()
        x = jnp.where(is_prompt, x_refs[0][...], x_refs[1][...]) if split_in else x_refs[0][...]
        if with_mix:
            mix = jnp.where(is_prompt, mixp_ref[...], mixs_ref[...])
            x = x + g2_ref[pl.ds(row, 1), :] * _dot(mix, wo_ref[...])
        x_scr[...] = x
        y = x * lax.rsqrt(jnp.mean(x * x, axis=-1, keepdims=True) + EPS) * gn_ref[...]
        h = (y * (1.0 + sc_ref[pl.ds(row, 1), :]) + sh_ref[pl.ds(row, 1), :]).astype(BF16)
        for c in range(N_FF_CHUNKS):
            lanes = slice(c * FF_CHUNK, (c + 1) * FF_CHUNK)
            if first:
                slot = c % 2
                for cp in weight_copies(c, slot):
                    cp.wait()
                wg_scr[:, lanes] = stg_g[slot].astype(BF16)
                wu_scr[:, lanes] = stg_u[slot].astype(BF16)
                wd_scr[lanes, :] = stg_d[slot].astype(BF16)
                if c + 2 < N_FF_CHUNKS:
                    for cp in weight_copies(c + 2, slot):
                        cp.start()
            gate = _dot(h, wg_scr[:, lanes])
            up = _dot(h, wu_scr[:, lanes])
            a_scr[:, lanes] = ((gate * jax.nn.sigmoid(gate)) * up).astype(BF16)
        out = x_scr[...] + (0.5 * g_ref[pl.ds(row, 1), :]) * _dot(a_scr[...], wd_scr[...])
        if not split_out:
            o_refs[0][...] = out
        elif first:
            o_refs[0][...] = out
        else:
            @pl.when(is_prompt)
            def _():
                o_refs[0][...] = out

            @pl.when(jnp.logical_not(is_prompt))
            def _():
                o_refs[1][...] = out

    pl.when(b == 0)(functools.partial(block, True))
    pl.when(b != 0)(functools.partial(block, False))


def _ffn_call(x, mod, layer, slot, g_norm, w_gate, w_up, w_down, *, n_prompt, mix=None, w_o=None, split_out=False):
    split_in = isinstance(x, tuple)
    t = x[0].shape[0] + x[1].shape[0] if split_in else x.shape[0]
    tm = TOK_BLOCK
    npb = n_prompt // tm
    with_mix = mix is not None
    mod_spec = lambda k: pl.BlockSpec((None, MOD_ROWS, D_MODEL), lambda b: (layer, 0, k))
    x_spec = pl.BlockSpec((tm, D_MODEL), lambda b: (b, 0))
    p_spec = pl.BlockSpec((tm, D_MODEL), lambda b: (jnp.minimum(b, npb - 1), 0))
    s_spec = pl.BlockSpec((tm, D_MODEL), lambda b: (jnp.maximum(b - npb, 0), 0))
    hbm = pl.BlockSpec(memory_space=pl.ANY)
    in_specs = [p_spec, s_spec] if split_in else [x_spec]
    args = list(x) if split_in else [x]
    if with_mix:
        in_specs += [p_spec, s_spec, pl.BlockSpec((None, D_MODEL, D_MODEL), lambda b: (layer, 0, 0)), mod_spec(5)]
        args += [mix[0], mix[1], w_o, mod]
    c0 = 3 * slot
    in_specs += [mod_spec(c0), mod_spec(c0 + 1), mod_spec(c0 + 2),
                 pl.BlockSpec((None, 1, D_MODEL), lambda b: (layer, 0, 0)), hbm, hbm, hbm]
    args += [mod, mod, mod, g_norm.reshape(DEPTH, 1, D_MODEL), w_gate, w_up, w_down]
    scratch = [pltpu.VMEM((D_MODEL, D_FF), BF16), pltpu.VMEM((D_MODEL, D_FF), BF16), pltpu.VMEM((D_FF, D_MODEL), BF16),
               pltpu.VMEM((2, D_MODEL, FF_CHUNK), F32), pltpu.VMEM((2, D_MODEL, FF_CHUNK), F32),
               pltpu.VMEM((2, FF_CHUNK, D_MODEL), F32), pltpu.SemaphoreType.DMA((3, 2)),
               pltpu.VMEM((tm, D_MODEL), F32), pltpu.VMEM((tm, D_FF), BF16)]
    if split_out:
        out_specs = [p_spec, s_spec]
        out_shape = [jax.ShapeDtypeStruct((n_prompt, D_MODEL), F32), jax.ShapeDtypeStruct((t - n_prompt, D_MODEL), F32)]
    else:
        out_specs = x_spec
        out_shape = jax.ShapeDtypeStruct((t, D_MODEL), F32)
    return pl.pallas_call(
        functools.partial(_ffn_kernel, layer=layer, n_prompt_blocks=npb, blocks_per_batch=DEC_SEQ // tm,
                          with_mix=with_mix, split_in=split_in, split_out=split_out),
        grid=(t // tm,),
        in_specs=in_specs,
        out_specs=out_specs,
        out_shape=out_shape,
        scratch_shapes=scratch,
        compiler_params=_cparams(("arbitrary",)),
        name="ffn_mix" if with_mix else "ffn",
    )(*args)


def _swap(x, width):
    lane = lax.broadcasted_iota(jnp.int32, x.shape, 1)
    first = (lane & width) == 0
    return jnp.where(first, pltpu.roll(x, LANES - width, 1), pltpu.roll(x, width, 1))


def _rope(x, cos, sin, width):
    parts = []
    for c in range(x.shape[1] // LANES):
        xc = x[:, c * LANES:(c + 1) * LANES]
        parts.append(xc * cos + _swap(xc, width) * sin)
    return parts[0] if len(parts) == 1 else jnp.concatenate(parts, axis=1)


def _head_norm(x, ones, gain, inv_dim):
    ms = _group_sums(x * x, ones) * inv_dim
    return x * lax.rsqrt(ms + EPS) * gain


def _mla_keys(ckv_n_bf16, kr_pad, w_ukv, ones128, kn_a):
    kv = _dot(ckv_n_bf16, w_ukv)
    k = kv[:, :512] + jnp.concatenate([kr_pad] * N_HEADS, axis=1)
    return _head_norm(k, ones128, kn_a, 1.0 / A_QK), kv[:, 512:]


def _prep_kernel(*refs, n_prompt_blocks, blocks_per_batch, n_alias):
    (x_ref, sh_ref, sc_ref, gn_ref, win_ref, gqa_ref, wuq_ref, gkva_ref, wukv_ref,
     qna_ref, kna_ref, qnb_ref, knb_ref, qnd_ref, knd_ref,
     ones128_ref, ones64_ref, cos_a_ref, sin_a_ref, cos_b_ref, sin_b_ref) = refs[:21]
    feat_ref, ckv_ref, kr_ref, wk_ref, wv_ref, nk_ref, nv_ref, win_scr = refs[21 + n_alias:]
    b = pl.program_id(0)
    row = _mod_row(b, n_prompt_blocks, blocks_per_batch)
    is_prompt = b < n_prompt_blocks

    @pl.when(b == 0)
    def _():
        win_scr[0:P_KR + A_ROPE, :] = win_ref[0:W_IN_KR + A_ROPE, :].astype(BF16)
        win_scr[P_KR + A_ROPE:P_QB, :] = jnp.zeros((LANES - A_ROPE, D_MODEL), BF16)
        for p, hd in enumerate(B_ORDER):
            win_scr[P_QB + HEAD_DIM * p:P_QB + HEAD_DIM * (p + 1), :] = (
                win_ref[W_IN_QB + HEAD_DIM * hd:W_IN_QB + HEAD_DIM * (hd + 1), :].astype(BF16))
        for r in range(W_IN_KB, W_IN_WIDTH, PREP_BLOCK):
            win_scr[r + P_KB - W_IN_KB:r + P_KB - W_IN_KB + PREP_BLOCK, :] = win_ref[r:r + PREP_BLOCK, :].astype(BF16)

    x = x_ref[...]
    y = x * lax.rsqrt(jnp.mean(x * x, axis=-1, keepdims=True) + EPS) * gn_ref[...]
    h = (y * (1.0 + sc_ref[pl.ds(row, 1), :]) + sh_ref[pl.ds(row, 1), :]).astype(BF16)
    proj = _dot_t(h, win_scr[...])
    ones128 = ones128_ref[...]
    ones64 = ones64_ref[...]

    cq = proj[:, P_CQ:P_CQ + 256]
    cqn = cq * lax.rsqrt(jnp.mean(cq * cq, axis=-1, keepdims=True) + EPS) * gqa_ref[...]
    qa = _head_norm(_dot(cqn.astype(BF16), wuq_ref[...]), ones128, qna_ref[...], 1.0 / A_QK)
    ckv = proj[:, P_CKV:P_CKV + 128]
    ckv_n = ckv * lax.rsqrt(jnp.mean(ckv * ckv, axis=-1, keepdims=True) + EPS) * gkva_ref[...]
    kr_pad = proj[:, P_KR:P_KR + 128]
    ka, va = _mla_keys(ckv_n.astype(BF16), kr_pad, wukv_ref[...], ones128, kna_ref[...])
    cos_a, sin_a = cos_a_ref[...], sin_a_ref[...]
    qa = _rope(qa, cos_a, sin_a, 8)
    ka = _rope(ka, cos_a, sin_a, 8)
    feat_ref[:, QA:QA + 512] = (qa * (A_QK ** -0.5)).astype(BF16)
    feat_ref[:, KA:KA + 512] = ka.astype(BF16)
    feat_ref[:, VA:VA + 256] = va.astype(BF16)

    qb = _head_norm(proj[:, P_QB:P_QB + 256], ones64, qnb_ref[...], 1.0 / HEAD_DIM)
    kb = _head_norm(proj[:, P_KB:P_KB + 128], ones64[:128, :128], knb_ref[:, :128], 1.0 / HEAD_DIM)
    vb = proj[:, P_VB:P_VB + 128]
    cos_b, sin_b = cos_b_ref[...], sin_b_ref[...]
    qb = _rope(qb, cos_b, sin_b, 16)
    kb = _rope(kb, cos_b, sin_b, 16)
    feat_ref[:, QB:QB + 256] = (qb * (HEAD_DIM ** -0.5)).astype(BF16)
    feat_ref[:, KB:KB + 128] = kb.astype(BF16)
    feat_ref[:, VB:VB + 128] = vb.astype(BF16)

    feat_ref[:, XC:XC + 256] = proj[:, P_XC:P_XC + 256].astype(BF16)

    qd = _head_norm(proj[:, P_QD:P_QD + 256], ones64, qnd_ref[...], 1.0 / HEAD_DIM)
    kd = _head_norm(proj[:, P_KD:P_KD + 256], ones64, knd_ref[...], 1.0 / HEAD_DIM)
    vd = proj[:, P_VD:P_VD + 256]
    feat_ref[:, QD:QD + 256] = (qd * (HEAD_DIM ** -0.5)).astype(BF16)
    feat_ref[:, KD:KD + 256] = kd.astype(BF16)
    feat_ref[:, VD:VD + 256] = vd.astype(BF16)

    @pl.when(is_prompt)
    def _():
        outs = ((ckv_ref, ckv_n), (kr_ref, kr_pad[:, :A_ROPE]), (wk_ref, kb), (wv_ref, vb), (nk_ref, kd), (nv_ref, vd))
        for ref, val in outs:
            if n_alias:
                ref[...] = val
            else:
                ref[0] = val
                ref[1:] = jnp.zeros((DEPTH - 1,) + val.shape, F32)


_PREP_WEIGHTS = ("g_mix", "w_in", "g_qa", "w_uq", "g_kva", "w_ukv",
                 "qn_a", "kn_a", "qn_b", "kn_b", "qn_d", "kn_d")
_CACHE_WIDTHS = (128, A_ROPE, 128, 128, 256, 256)


def _prep_call(x, mod, layer, lw, consts, new_caches, *, n_prompt):
    t = x.shape[0]
    tm = PREP_BLOCK
    npb = n_prompt // tm
    bpb = DEC_SEQ // tm
    batch = n_prompt // SEQ
    full = lambda a: pl.BlockSpec(a.shape, lambda b: (0,) * a.ndim)
    mod_spec = lambda k: pl.BlockSpec((None, MOD_ROWS, D_MODEL), lambda b: (layer, 0, k))
    lyr = lambda a: pl.BlockSpec((None,) + a.shape[1:], lambda b: (layer,) + (0,) * (a.ndim - 1))
    tok = lambda w: pl.BlockSpec((tm, w), lambda b: (b, 0))
    pos = pl.BlockSpec((tm, LANES), lambda b: (jnp.where(b < npb, 0, 1 + (b - npb) % bpb), 0))
    weights = [lw[k] for k in _PREP_WEIGHTS]
    tabs = [consts["cos_a"], consts["sin_a"], consts["cos_b"], consts["sin_b"]]
    in_specs = [tok(D_MODEL), mod_spec(3), mod_spec(4)] + [lyr(a) for a in weights]
    in_specs += [full(consts["ones128"]), full(consts["ones64"])] + [pos] * len(tabs)
    args = [x, mod, mod] + weights + [consts["ones128"], consts["ones64"]] + tabs
    new_caches = [] if new_caches is None else list(new_caches)
    aliases = {len(args) + k: 1 + k for k in range(len(new_caches))}
    in_specs += [pl.BlockSpec(memory_space=pl.ANY)] * len(new_caches)
    args += new_caches
    spb = SEQ // tm
    layer_dim, layer_idx = (None, layer) if new_caches else (DEPTH, 0)
    cache_spec = lambda w: pl.BlockSpec(
        (None, layer_dim, tm, w),
        lambda b: (jnp.minimum(b, npb - 1) // spb, layer_idx, jnp.minimum(b, npb - 1) % spb, 0))
    out_specs = [tok(FEAT)] + [cache_spec(w) for w in _CACHE_WIDTHS]
    out_shape = [jax.ShapeDtypeStruct((t, FEAT), BF16)]
    out_shape += [jax.ShapeDtypeStruct((batch, DEPTH, SEQ, w), F32) for w in _CACHE_WIDTHS]
    feat, *caches = pl.pallas_call(
        functools.partial(_prep_kernel, n_prompt_blocks=npb, blocks_per_batch=bpb, n_alias=len(aliases)),
        grid=(t // tm,),
        in_specs=in_specs,
        out_specs=out_specs,
        out_shape=out_shape,
        input_output_aliases=aliases,
        scratch_shapes=[pltpu.VMEM((P_WIDTH, D_MODEL), BF16)],
        compiler_params=_cparams(("arbitrary",)),
        name="prep",
    )(*args)
    return feat, caches


def _slot_mask(shape, width, slot):
    lane = lax.broadcasted_iota(jnp.int32, shape, 1)
    return (lane >= slot * width) & (lane < (slot + 1) * width)


def _softmax_pv(scores, values, slot, sink=None):
    m = functools.reduce(jnp.maximum, [jnp.max(s, axis=-1, keepdims=True) for s in scores])
    if sink is not None:
        m = jnp.maximum(m, sink)
    one = jnp.ones((), BF16)
    o = functools.reduce(jnp.add, [
        _dot(jnp.exp(s - m).astype(BF16), jnp.where(_slot_mask(v.shape, HEAD_DIM, slot), v, one))
        for s, v in zip(scores, values)])
    groups = [pltpu.roll(o[:, c * LANES:(c + 1) * LANES], HEAD_DIM, 1) for c in range(o.shape[1] // LANES)]
    den = groups[0] if len(groups) == 1 else jnp.concatenate(groups, axis=1)
    if sink is not None:
        den = den + jnp.exp(sink - m)
    return o / den


def _attn_ctx_kernel(sink_ref, feat_ref, cl_ref, sl_ref, ccb_ref, scb_ref, o_ref):
    n = feat_ref.shape[0]
    zero = jnp.zeros((), BF16)

    va = feat_ref[:, VA:VA + 256]
    oa = jnp.zeros((n, 256), F32)
    for h in range(N_HEADS):
        q = feat_ref[:, QA + 128 * h:QA + 128 * (h + 1)]
        k = feat_ref[:, KA + 128 * h:KA + 128 * (h + 1)]
        o = _softmax_pv([_dot_t(q, k)], [va], h)
        oa = jnp.where(_slot_mask(o.shape, HEAD_DIM, h), o, oa)
    o_ref[:, 0:256] = oa.astype(BF16)

    kb = feat_ref[:, KB:KB + 128]
    vb = feat_ref[:, VB:VB + 128]
    for c in range(2):
        qc = feat_ref[:, QB + 128 * c:QB + 128 * (c + 1)]
        ob = jnp.zeros((n, 128), F32)
        for half in range(2):
            head = B_ORDER[2 * c + half]
            msk = _slot_mask(qc.shape, HEAD_DIM, half)
            o = _softmax_pv([_dot_t(jnp.where(msk, qc, zero), kb)], [vb], half, sink=sink_ref[head])
            ob = jnp.where(msk, o, ob)
        o_ref[:, 256 + 128 * c:256 + 128 * (c + 1)] = ob.astype(BF16)

    xc = feat_ref[:, XC:XC + 256]
    fa = _dot(xc, ccb_ref[...]).astype(BF16)
    fb = _dot(xc, scb_ref[...]).astype(BF16)
    o_ref[:, 512:768] = (_dot(cl_ref[...], fa) - _dot(sl_ref[...], fb)).astype(BF16)

    qd = feat_ref[:, QD:QD + 256]
    kd = feat_ref[:, KD:KD + 256]
    vd = feat_ref[:, VD:VD + 256]
    od = jnp.zeros((n, 256), F32)
    for h in range(N_HEADS):
        msk = _slot_mask(qd.shape, HEAD_DIM, h)
        o = _softmax_pv([_dot_t(jnp.where(msk, qd, zero), kd)], [vd], h)
        od = jnp.where(msk, o, od)
    o_ref[:, 768:1024] = od.astype(BF16)


def _attn_ctx_call(feat, sink, consts, *, n_prompt):
    t = n_prompt
    full = lambda a: pl.BlockSpec(a.shape, lambda i: (0,) * a.ndim)
    tabs = [consts["cl_ctx"], consts["sl_ctx"], consts["ccb_ctx"], consts["scb_ctx"]]
    return pl.pallas_call(
        _attn_ctx_kernel,
        grid=(t // SEQ,),
        in_specs=[pl.BlockSpec(memory_space=pltpu.SMEM),
                  pl.BlockSpec((SEQ, FEAT), lambda i: (i, 0))] + [full(a) for a in tabs],
        out_specs=pl.BlockSpec((SEQ, D_MODEL), lambda i: (i, 0)),
        out_shape=jax.ShapeDtypeStruct((t, D_MODEL), BF16),
        compiler_params=_cparams(("parallel",)),
        name="attn_ctx",
    )(sink, feat, *tabs)


def _attn_lat_kernel(sink_ref, feat_ref, cckv_ref, ckr_ref, cwk_ref, cwv_ref, cnk_ref, cnv_ref,
                     wukv_ref, kna_ref, ones128_ref, tt_ref, cl_ref, sl_ref, ccb_ref, scb_ref,
                     o_ref, kac_scr, vac_scr, fa_scr, fb_scr):
    j = pl.program_id(1)
    nq = Q_BLOCK
    zero = jnp.zeros((), BF16)

    @pl.when(j == 0)
    def _():
        ka_c, va_c = _mla_keys(cckv_ref[...].astype(BF16), ckr_ref[...], wukv_ref[...],
                               ones128_ref[...], kna_ref[...])
        kac_scr[...] = ka_c.astype(BF16)
        vac_scr[...] = va_c.astype(BF16)
        xc = feat_ref[:, XC:XC + 256]
        fa_scr[...] = _dot(xc, ccb_ref[...]).astype(BF16)
        fb_scr[...] = _dot(xc, scb_ref[...]).astype(BF16)

    r0 = pl.multiple_of(j * nq, nq)
    rows = pl.ds(r0, nq)

    va = feat_ref[:, VA:VA + 256]
    va_c = vac_scr[...]
    oa = jnp.zeros((nq, 256), F32)
    for h in range(N_HEADS):
        q = feat_ref[rows, QA + 128 * h:QA + 128 * (h + 1)]
        s_c = _dot_t(q, kac_scr[:, 128 * h:128 * (h + 1)])
        s_l = _dot_t(q, feat_ref[:, KA + 128 * h:KA + 128 * (h + 1)])
        o = _softmax_pv([s_c, s_l], [va_c, va], h)
        oa = jnp.where(_slot_mask(o.shape, HEAD_DIM, h), o, oa)
    o_ref[:, 0:256] = oa.astype(BF16)

    start = pl.multiple_of(jnp.clip(r0 - B_WINDOW, 0, DEC_SEQ - B_KEYS), B_WINDOW)
    kw = feat_ref[pl.ds(start, B_KEYS), KB:KB + 128]
    vw = feat_ref[pl.ds(start, B_KEYS), VB:VB + 128]
    kb_c = cwk_ref[...].astype(BF16)
    vb_c = cwv_ref[...].astype(BF16)
    qpos = r0 + lax.broadcasted_iota(jnp.int32, (nq, B_KEYS), 0)
    kpos = start + lax.broadcasted_iota(jnp.int32, (nq, B_KEYS), 1)
    band = jnp.abs(qpos - kpos) <= B_WINDOW
    for c in range(2):
        qc = feat_ref[rows, QB + 128 * c:QB + 128 * (c + 1)]
        ob = jnp.zeros((nq, 128), F32)
        for half in range(2):
            head = B_ORDER[2 * c + half]
            msk = _slot_mask(qc.shape, HEAD_DIM, half)
            q = jnp.where(msk, qc, zero)
            s_l = jnp.where(band, _dot_t(q, kw), NEG)
            s_c = _dot_t(q, kb_c)
            o = _softmax_pv([s_l, s_c], [vw, vb_c], half, sink=sink_ref[head])
            ob = jnp.where(msk, o, ob)
        o_ref[:, 256 + 128 * c:256 + 128 * (c + 1)] = ob.astype(BF16)

    o_ref[:, 512:768] = (_dot(cl_ref[rows, :], fa_scr[...]) - _dot(sl_ref[rows, :], fb_scr[...])).astype(BF16)

    rows_per_block = nq // GRID_W
    n_rows = DEC_SEQ // GRID_W
    kr_win = min(NA_ROWS, n_rows)
    ks = jnp.where(j < 2, 0, n_rows - D_KEY_ROWS)
    k0_tok = pl.multiple_of(ks * GRID_W, GRID_W)
    keys = pl.ds(k0_tok, D_KEY_ROWS * GRID_W)
    kdw = feat_ref[keys, KD:KD + 256]
    vdw = feat_ref[keys, VD:VD + 256]
    kd_c = cnk_ref[...].astype(BF16)
    vd_c = cnv_ref[...].astype(BF16)
    qd = feat_ref[rows, QD:QD + 256]
    first = lax.broadcasted_iota(jnp.int32, (GRID_W, LANES), 1) < GRID_W
    od = jnp.zeros((nq, 256), F32)
    for h in range(N_HEADS):
        bias_rows = []
        for rr in range(rows_per_block):
            r = j * rows_per_block + rr
            rs = jnp.clip(r - kr_win // 2, 0, n_rows - kr_win)
            tiles = []
            for p in range(D_KEY_ROWS // 2):
                ka0 = ks + 2 * p
                e = jnp.clip(ka0 - r + NA_ROWS, 0, 2 * NA_ROWS - 1)
                ok0 = ((ka0 >= rs) & (ka0 < rs + kr_win)).astype(jnp.int32)
                ok1 = ((ka0 + 1 >= rs) & (ka0 + 1 < rs + kr_win)).astype(jnp.int32)
                ok = jnp.where(first, ok0, ok1) > 0
                tiles.append(jnp.where(ok, tt_ref[h, e], NEG))
            bias_rows.append(jnp.concatenate(tiles, axis=1))
        bias = jnp.concatenate(bias_rows, axis=0)
        msk = _slot_mask(qd.shape, HEAD_DIM, h)
        q = jnp.where(msk, qd, zero)
        s_l = _dot_t(q, kdw) + bias
        s_c = _dot_t(q, kd_c)
        o = _softmax_pv([s_l, s_c], [vdw, vd_c], h)
        od = jnp.where(msk, o, od)
    o_ref[:, 768:1024] = od.astype(BF16)


def _attn_lat_call(feat, sink, layer, caches, lw, consts, tt, *, n_prompt):
    first = n_prompt // DEC_SEQ
    nb = feat.shape[0] // DEC_SEQ - first
    full = lambda a: pl.BlockSpec(a.shape, lambda b, j: (0,) * a.ndim)
    lyr = lambda a: pl.BlockSpec((None,) + a.shape[1:], lambda b, j: (layer,) + (0,) * (a.ndim - 1))
    cache = lambda a: pl.BlockSpec((None, None) + a.shape[2:], lambda b, j: (b, layer, 0, 0))
    tabs = [consts["cl_lat"], consts["sl_lat"], consts["ccb_lat"], consts["scb_lat"]]
    in_specs = [pl.BlockSpec(memory_space=pltpu.SMEM),
                pl.BlockSpec((DEC_SEQ, FEAT), lambda b, j: (first + b, 0))]
    in_specs += [cache(a) for a in caches]
    in_specs += [lyr(lw["w_ukv"]), lyr(lw["kn_a"]), full(consts["ones128"]),
                 pl.BlockSpec((N_HEADS,) + tt.shape[1:], lambda b, j: (layer, 0, 0, 0))]
    in_specs += [full(a) for a in tabs]
    return pl.pallas_call(
        _attn_lat_kernel,
        grid=(nb, DEC_SEQ // Q_BLOCK),
        in_specs=in_specs,
        out_specs=pl.BlockSpec((Q_BLOCK, D_MODEL), lambda b, j: (b * (DEC_SEQ // Q_BLOCK) + j, 0)),
        out_shape=jax.ShapeDtypeStruct((nb * DEC_SEQ, D_MODEL), BF16),
        scratch_shapes=[pltpu.VMEM((PAST_LEN, 512), BF16), pltpu.VMEM((PAST_LEN, 256), BF16),
                        pltpu.VMEM((DEC_SEQ, 256), BF16), pltpu.VMEM((DEC_SEQ, 256), BF16)],
        compiler_params=_cparams(("parallel", "arbitrary")),
        name="attn_lat",
    )(sink, feat, *caches, lw["w_ukv"], lw["kn_a"], consts["ones128"], tt, *tabs)


def _layer_weights(w_in, w_uq, w_ukv, w_o, g_mix, g_qa, g_kva, qn_a, kn_a, qn_b, kn_b, qn_d, kn_d):
    z = lambda *s: jnp.zeros(s, F32)

    def slots(a, rows):
        parts = []
        for h in range(N_HEADS):
            hd = a[..., A_QK * h:A_QK * (h + 1)]
            parts += [hd[..., HEAD_DIM:], z(*rows, LANES - A_QK), hd[..., :HEAD_DIM]]
        return jnp.concatenate(parts, axis=-1)

    w_uq_p = slots(w_uq, (DEPTH, 256)).astype(BF16)
    k_cols, v_cols = [], []
    for h in range(N_HEADS):
        k_cols += [z(DEPTH, 128, HEAD_DIM), w_ukv[..., 128 * h:128 * h + HEAD_DIM]]
        v_cols.append(w_ukv[..., 128 * h + HEAD_DIM:128 * (h + 1)])
    w_ukv_p = jnp.concatenate(k_cols + v_cols, axis=-1).astype(BF16)
    w_o_p = jnp.concatenate(
        [w_o[:, :256]] + [w_o[:, 256 + HEAD_DIM * h:256 + HEAD_DIM * (h + 1)] for h in B_ORDER]
        + [w_o[:, 512:]], axis=1).astype(BF16)
    gain_a = lambda g: slots(jnp.tile(g, (1, N_HEADS)), (DEPTH,))[:, None, :]
    tile4 = lambda g: jnp.tile(g, (1, N_HEADS))[:, None, :]
    lw = dict(w_in=jnp.swapaxes(w_in, 1, 2), w_uq=w_uq_p, w_ukv=w_ukv_p,
              g_mix=g_mix[:, None, :], g_qa=g_qa[:, None, :], g_kva=g_kva[:, None, :],
              qn_a=gain_a(qn_a), kn_a=gain_a(kn_a), qn_b=tile4(qn_b), kn_b=tile4(kn_b),
              qn_d=tile4(qn_d), kn_d=tile4(kn_d))
    return lw, w_o_p


def _constants():
    consts = {}
    ident = lambda cs: (jnp.concatenate([jnp.ones((PREP_BLOCK, LANES), F32), cs[0]]),
                        jnp.concatenate([jnp.zeros((PREP_BLOCK, LANES), F32), cs[1]]))
    consts["cos_a"], consts["sin_a"] = ident(_rope_tables(A_ROPE, LANES, DEC_SEQ))
    consts["cos_b"], consts["sin_b"] = ident(_rope_tables(HEAD_DIM, HEAD_DIM, DEC_SEQ))
    for tag, n in (("ctx", SEQ), ("lat", DEC_SEQ)):
        cl, sl, ccb, scb = _fourier_constants(n)
        consts["cl_" + tag], consts["sl_" + tag], consts["ccb_" + tag], consts["scb_" + tag] = cl, sl, ccb, scb
    consts["ones128"] = _group_ones(512, LANES)
    consts["ones64"] = _group_ones(256, HEAD_DIM)
    return consts


def kernel(x_prompt, x_sample, cache_mla_ckv, cache_mla_krope, cache_win_k, cache_win_v, cache_na_k, cache_na_v, c, c_ctx, w_ada, b_ada, g_ffn1, w_gate1, w_up1, w_down1, g_mix, w_in, g_qa, w_uq, g_kva, w_ukv, qn_a, kn_a, qn_b, kn_b, sink_b, qn_d, kn_d, rpb_d, w_o, g_ffn2, w_gate2, w_up2, w_down2):
    batch, seq, _ = x_prompt.shape
    dec_batch, dec_seq, _ = x_sample.shape
    assert (seq, dec_seq, dec_batch + 1 <= MOD_ROWS) == (SEQ, DEC_SEQ, True)

    consts = _constants()
    lw, w_o_p = _layer_weights(w_in, w_uq, w_ukv, w_o, g_mix, g_qa, g_kva, qn_a, kn_a, qn_b, kn_b, qn_d, kn_d)

    cond = jnp.concatenate([c_ctx[None, :], c, jnp.zeros((MOD_ROWS - 1 - dec_batch, D_MODEL), F32)], axis=0)
    mod = _ada_call(cond, w_ada, b_ada)
    tt = _rpb_call(rpb_d)

    caches = [cache_mla_ckv,
              jnp.pad(cache_mla_krope, ((0, 0), (0, 0), (0, 0), (0, LANES - A_ROPE))),
              cache_win_k.reshape(dec_batch, DEPTH, PAST_LEN, 128),
              cache_win_v.reshape(dec_batch, DEPTH, PAST_LEN, 128),
              cache_na_k.reshape(dec_batch, DEPTH, PAST_LEN, 256),
              cache_na_v.reshape(dec_batch, DEPTH, PAST_LEN, 256)]

    n_prompt = batch * seq
    x = (x_prompt.reshape(n_prompt, D_MODEL), x_sample.reshape(dec_batch * dec_seq, D_MODEL))
    new = None
    for l in range(DEPTH):
        sink = sink_b[l]
        x = _ffn_call(x, mod, l, 0, g_ffn1, w_gate1, w_up1, w_down1, n_prompt=n_prompt)
        feat, new = _prep_call(x, mod, l, lw, consts, new, n_prompt=n_prompt)
        mix = (_attn_ctx_call(feat, sink, consts, n_prompt=n_prompt),
               _attn_lat_call(feat, sink, l, caches, lw, consts, tt, n_prompt=n_prompt))
        x = _ffn_call(x, mod, l, 2, g_ffn2, w_gate2, w_up2, w_down2, n_prompt=n_prompt, mix=mix, w_o=w_o_p,
                      split_out=l == DEPTH - 1)

    heads = lambda a, n: a.reshape(batch, DEPTH, seq, n, HEAD_DIM)
    return (x[0].reshape(batch, seq, D_MODEL), x[1].reshape(dec_batch, dec_seq, D_MODEL),
            new[0], new[1], heads(new[2], 2), heads(new[3], 2), heads(new[4], N_HEADS), heads(new[5], N_HEADS))
```
